```python
import math
import jax, jax.numpy as jnp
from jax import lax
import numpy as np

D_MODEL = 1024
BATCH = 4
SEQ = 8192
DEPTH = 2

CHUNK = 64
HEAD_DIM = 64
RWKV_HEADS = 8
SB_HEADS = 8
RWKV_W = RWKV_HEADS * HEAD_DIM
SB_W = SB_HEADS * HEAD_DIM
D_MIX = RWKV_W + SB_W
DECAY_RANK = 32
ICLR_RANK = 32
VRES_RANK = 32
GATE_RANK = 96
D_FF = 3584
N_EXPERTS = 8
TOP_K = 2
SB_BLOCK = 128
NORM_EPS = 1e-6
LNX_EPS = 64e-5
DECAY_SCALE = math.exp(-0.5)

kernel_name = 'hybrid_rwkv7_stickbreaking_moe_block'


def _rwkv_cols(has_vres):
    return 3 * RWKV_W + DECAY_RANK + ICLR_RANK + GATE_RANK + (VRES_RANK if has_vres else 0)


def _rmsnorm(x, g):
    xf = x.astype(jnp.float32)
    y = xf * lax.rsqrt(jnp.mean(xf * xf, axis=-1, keepdims=True) + NORM_EPS)
    return (y * g.astype(jnp.float32)).astype(x.dtype)


def _heads(t, n_heads):
    return t.reshape(t.shape[:-1] + (n_heads, t.shape[-1] // n_heads))


def _token_shift(p, mu):
    prev = jnp.pad(p, ((0, 0), (1, 0), (0, 0)))[:, :-1]
    return p + mu * (prev - p)


def _rwkv7_scan(r, w, k, v, z, b):
    def step(S, inp):
        r_t, w_t, k_t, v_t, z_t, b_t = inp
        Sz = jnp.einsum('bhvk,bhk->bhv', S, z_t)
        S = S * w_t[:, :, None, :] + Sz[..., None] * b_t[:, :, None, :] + v_t[..., None] * k_t[:, :, None, :]
        return S, jnp.einsum('bhvk,bhk->bhv', S, r_t)
    B, T, H, N = r.shape
    xs = (jnp.moveaxis(r, 1, 0), jnp.moveaxis(w, 1, 0), jnp.moveaxis(k, 1, 0),
          jnp.moveaxis(v, 1, 0), jnp.moveaxis(z, 1, 0), jnp.moveaxis(b, 1, 0))
    _, ys = lax.scan(step, jnp.zeros((B, H, N, N), jnp.float32), xs)
    return jnp.moveaxis(ys, 0, 1)


def _rwkv7_group(proj, p, v_first):
    f32 = jnp.float32
    has_vres = v_first is not None
    sizes = [RWKV_W] * 3 + [DECAY_RANK, ICLR_RANK, GATE_RANK] + ([VRES_RANK] if has_vres else [])
    parts = jnp.split(_token_shift(proj, p['mu']), [int(o) for o in np.cumsum(sizes)[:-1]], axis=-1)
    r, k, v, w_dn, a_dn, g_dn = parts[:6]
    decay = jnp.exp(-DECAY_SCALE * jax.nn.sigmoid((p['decay0'] + jnp.tanh(w_dn) @ p['decay_up']).astype(f32)))
    iclr = jax.nn.sigmoid((p['iclr0'] + a_dn @ p['iclr_up']).astype(f32))
    gate = (jax.nn.sigmoid(g_dn) @ p['gate_up']).astype(f32)
    if has_vres:
        v = v + (v_first - v) * jax.nn.sigmoid(p['vres0'] + parts[6] @ p['vres_up'])
    else:
        v_first = v
    r, k, v = r.astype(f32), k.astype(f32), v.astype(f32)
    kk = _heads(k * p['k_k'], RWKV_HEADS)
    kk = kk * lax.rsqrt(jnp.maximum(jnp.sum(kk * kk, axis=-1, keepdims=True), 1e-12))
    k = k * (1.0 + (iclr - 1.0) * p['k_a'])
    rh, kh, vh = _heads(r, RWKV_HEADS), _heads(k, RWKV_HEADS), _heads(v, RWKV_HEADS)
    wh, ah = _heads(decay, RWKV_HEADS), _heads(iclr, RWKV_HEADS)
    y = _rwkv7_scan(rh, wh, kh, vh, -kk, kk * ah)
    mean = jnp.mean(y, axis=-1, keepdims=True)
    var = jnp.mean(jnp.square(y - mean), axis=-1, keepdims=True)
    y = ((y - mean) * lax.rsqrt(var + LNX_EPS)).reshape(r.shape) * p['lnx_g'] + p['lnx_b']
    bonus = (jnp.sum(rh * kh * p['r_k'], axis=-1, keepdims=True) * vh).reshape(r.shape)
    return (y + bonus) * gate, v_first


def _stick_breaking(q, k, v):
    B, T, H, N = q.shape
    f32 = jnp.float32
    qh = jnp.transpose(q, (0, 2, 1, 3)).astype(f32) * (N ** -0.5)
    kh = jnp.transpose(k, (0, 2, 1, 3)).astype(f32)
    vh = jnp.transpose(v, (0, 2, 1, 3)).astype(f32)
    nb = T // SB_BLOCK
    q_blocks = jnp.transpose(qh.reshape(B, H, nb, SB_BLOCK, N), (2, 0, 1, 3, 4))
    kpos = jnp.arange(T)

    def block(args):
        qb, i = args
        qpos = i * SB_BLOCK + jnp.arange(SB_BLOCK)
        z = jnp.einsum('bhqd,bhkd->bhqk', qb, kh)
        mask = kpos[None, :] < qpos[:, None]
        log_keep = jnp.where(mask, jax.nn.log_sigmoid(-z), 0.0)
        between = lax.cumsum(log_keep, axis=3, reverse=True) - log_keep
        att = jnp.where(mask, jnp.exp(jax.nn.log_sigmoid(z) + between), 0.0)
        return jnp.einsum('bhqk,bhkd->bhqd', att, vh)

    o = lax.map(block, (q_blocks, jnp.arange(nb)))
    return jnp.transpose(o, (1, 0, 3, 2, 4)).reshape(B, T, H, N)


def _mixer(h, p, v_first):
    proj = h @ p['w_in']
    n_rwkv = p['mu'].shape[0]
    y_a, v_first = _rwkv7_group(proj[..., :n_rwkv], p, v_first)
    q, k, v = jnp.split(proj[..., n_rwkv:], 3, axis=-1)
    o = _stick_breaking(_heads(q, SB_HEADS), _heads(k, SB_HEADS), _heads(v, SB_HEADS))
    o = o * lax.rsqrt(jnp.mean(o * o, axis=-1, keepdims=True) + NORM_EPS)
    y_b = o.reshape(h.shape[:-1] + (SB_W,)) * p['sb_g']
    y = jnp.concatenate([y_a, y_b], axis=-1).astype(h.dtype) @ p['w_out']
    return y, v_first


def _swiglu(h, w_gate, w_up, w_down):
    return (jax.nn.silu(h @ w_gate) * (h @ w_up)) @ w_down


def _moe(h, router, w_gate, w_up, w_down):
    B, T, D = h.shape
    hf = h.reshape(B * T, D)
    logits = (hf @ router).astype(jnp.float32)
    top_v, top_i = lax.top_k(logits, TOP_K)
    gates = jax.nn.softmax(top_v, axis=-1)
    dense_gate = jnp.sum(jax.nn.one_hot(top_i, N_EXPERTS, dtype=jnp.float32) * gates[..., None], axis=1)
    dense_gate = dense_gate.astype(h.dtype)
    out = jnp.zeros_like(hf)
    for e in range(N_EXPERTS):
        out = out + dense_gate[:, e:e + 1] * _swiglu(hf, w_gate[e], w_up[e], w_down[e])
    return out.reshape(B, T, D)


def _gain(key, n):
    return 1.0 + 0.05 * jax.random.normal(key, (n,), jnp.float32)


def _normal(key, shape, scale):
    return scale * jax.random.normal(key, shape, jnp.float32)


def _init_layer(key, l):
    ks = jax.random.split(key, 24)
    has_vres = l > 0
    n_rwkv = _rwkv_cols(has_vres)
    s = '_' + str(l)
    p = {}
    p['mix_pre' + s] = _gain(ks[0], D_MODEL)
    p['w_in' + s] = _normal(ks[1], (D_MODEL, n_rwkv + 3 * SB_W), D_MODEL ** -0.5)
    p['mu' + s] = jax.random.uniform(ks[2], (n_rwkv,), jnp.float32)
    p['decay0' + s] = _normal(ks[3], (RWKV_W,), 1.0)
    p['decay_up' + s] = _normal(ks[4], (DECAY_RANK, RWKV_W), DECAY_RANK ** -0.5)
    p['iclr0' + s] = _normal(ks[5], (RWKV_W,), 0.5)
    p['iclr_up' + s] = _normal(ks[6], (ICLR_RANK, RWKV_W), ICLR_RANK ** -0.5)
    p['gate_up' + s] = _normal(ks[7], (GATE_RANK, RWKV_W), GATE_RANK ** -0.5)
    if has_vres:
        p['vres0' + s] = _normal(ks[8], (RWKV_W,), 0.5)
        p['vres_up' + s] = _normal(ks[9], (VRES_RANK, RWKV_W), VRES_RANK ** -0.5)
    p['k_k' + s] = 0.85 + _normal(ks[10], (RWKV_W,), 0.05)
    p['k_a' + s] = 1.0 + _normal(ks[11], (RWKV_W,), 0.05)
    p['r_k' + s] = _normal(ks[12], (RWKV_HEADS, HEAD_DIM), 0.1)
    p['lnx_g' + s] = _gain(ks[13], RWKV_W)
    p['lnx_b' + s] = _normal(ks[14], (RWKV_W,), 0.02)
    p['sb_g' + s] = _gain(ks[15], SB_W)
    p['w_out' + s] = _normal(ks[16], (D_MIX, D_MODEL), D_MIX ** -0.5)
    p['mix_post' + s] = _gain(ks[17], D_MODEL)
    p['ffn_pre' + s] = _gain(ks[18], D_MODEL)
    if l % 2 == 0:
        p['ff_gate' + s] = _normal(ks[19], (D_MODEL, D_FF), D_MODEL ** -0.5)
        p['ff_up' + s] = _normal(ks[20], (D_MODEL, D_FF), D_MODEL ** -0.5)
        p['ff_down' + s] = _normal(ks[21], (D_FF, D_MODEL), D_FF ** -0.5)
    else:
        p['router' + s] = _normal(ks[19], (D_MODEL, N_EXPERTS), D_MODEL ** -0.5)
        p['moe_gate' + s] = _normal(ks[20], (N_EXPERTS, D_MODEL, D_FF), D_MODEL ** -0.5)
        p['moe_up' + s] = _normal(ks[21], (N_EXPERTS, D_MODEL, D_FF), D_MODEL ** -0.5)
        p['moe_down' + s] = _normal(ks[22], (N_EXPERTS, D_FF, D_MODEL), D_FF ** -0.5)
    p['ffn_post' + s] = _gain(ks[23], D_MODEL)
    return p


def setup_inputs(seed: int = 0) -> dict:
    key = jax.random.key(seed)
    ks = jax.random.split(key, DEPTH + 1)
    params = {'x': jax.random.normal(ks[0], (BATCH, SEQ, D_MODEL), jnp.float32)}
    for l in range(DEPTH):
        params.update(_init_layer(ks[l + 1], l))
    return params


def reference(x,
              mix_pre_0, w_in_0, mu_0, decay0_0, decay_up_0, iclr0_0, iclr_up_0, gate_up_0,
              k_k_0, k_a_0, r_k_0, lnx_g_0, lnx_b_0, sb_g_0, w_out_0, mix_post_0,
              ffn_pre_0, ff_gate_0, ff_up_0, ff_down_0, ffn_post_0,
              mix_pre_1, w_in_1, mu_1, decay0_1, decay_up_1, iclr0_1, iclr_up_1, gate_up_1,
              vres0_1, vres_up_1,
              k_k_1, k_a_1, r_k_1, lnx_g_1, lnx_b_1, sb_g_1, w_out_1, mix_post_1,
              ffn_pre_1, router_1, moe_gate_1, moe_up_1, moe_down_1, ffn_post_1):
    layers = [
        dict(mix_pre=mix_pre_0, w_in=w_in_0, mu=mu_0, decay0=decay0_0, decay_up=decay_up_0,
             iclr0=iclr0_0, iclr_up=iclr_up_0, gate_up=gate_up_0, k_k=k_k_0, k_a=k_a_0,
             r_k=r_k_0, lnx_g=lnx_g_0, lnx_b=lnx_b_0, sb_g=sb_g_0, w_out=w_out_0,
             mix_post=mix_post_0, ffn_pre=ffn_pre_0, ffn=(ff_gate_0, ff_up_0, ff_down_0),
             ffn_post=ffn_post_0),
        dict(mix_pre=mix_pre_1, w_in=w_in_1, mu=mu_1, decay0=decay0_1, decay_up=decay_up_1,
             iclr0=iclr0_1, iclr_up=iclr_up_1, gate_up=gate_up_1, vres0=vres0_1, vres_up=vres_up_1,
             k_k=k_k_1, k_a=k_a_1, r_k=r_k_1, lnx_g=lnx_g_1, lnx_b=lnx_b_1, sb_g=sb_g_1,
             w_out=w_out_1, mix_post=mix_post_1, ffn_pre=ffn_pre_1,
             ffn=(router_1, moe_gate_1, moe_up_1, moe_down_1), ffn_post=ffn_post_1),
    ]
    v_first = None
    for l in range(DEPTH):
        p = layers[l]
        y, v_first = _mixer(_rmsnorm(x, p['mix_pre']), p, v_first)
        x = x + _rmsnorm(y, p['mix_post'])
        hn = _rmsnorm(x, p['ffn_pre'])
        f = _swiglu(hn, *p['ffn']) if l % 2 == 0 else _moe(hn, *p['ffn'])
        x = x + _rmsnorm(f, p['ffn_post'])
    return x
```

```python
import functools
import math

import jax
import jax.numpy as jnp
from jax import lax
from jax.experimental import pallas as pl
from jax.experimental.pallas import tpu as pltpu

HEAD_DIM = 64
RWKV_W = 512
SB_W = 512
N_EXPERTS = 8
NORM_EPS = 1e-6
LNX_EPS = 64e-5
DECAY_SCALE = math.exp(-0.5)
LANES = 128
CHUNK = 64
LOW_SLOT = 128
VMEM_LIMIT = 56 * 1024 * 1024

F32 = jnp.float32
BF16 = jnp.bfloat16


def _dot(a, b):
    return jnp.dot(a, b, preferred_element_type=F32)


def _dot_nt(a, b):
    return lax.dot_general(a, b, (((1,), (1,)), ((), ())), preferred_element_type=F32)


def _dot_tn(a, b):
    return lax.dot_general(a, b, (((0,), (0,)), ((), ())), preferred_element_type=F32)


def _head_ones(width):
    r = lax.broadcasted_iota(jnp.int32, (width, width), 0) // HEAD_DIM
    c = lax.broadcasted_iota(jnp.int32, (width, width), 1) // HEAD_DIM
    return jnp.where(r == c, 1.0, 0.0).astype(BF16)


def _head_sum(x, ones_bd):
    hi = x.astype(BF16)
    lo = (x - hi.astype(F32)).astype(BF16)
    return _dot(hi, ones_bd) + _dot(lo, ones_bd)


def _mix_in_kernel(has_vres, x_ref, g_ref, w_ref, mu_ref, decay0_ref, decay_up_ref,
                   iclr0_ref, iclr_up_ref, gate_up_ref, vres0_ref, vres_up_ref,
                   kk_g_ref, ka_ref, rk_ref, vfirst_ref,
                   r_o, lw_o, k_o, v_o, kk_o, a_o, gate_o, bonus_o, q_o, sk_o, sv_o,
                   carry_ref):
    t = pl.program_id(1)
    n_shift = 3 * RWKV_W + 4 * LOW_SLOT

    @pl.when(t == 0)
    def _():
        carry_ref[...] = jnp.zeros_like(carry_ref)

    x = x_ref[0]
    hn = x * lax.rsqrt(jnp.mean(x * x, axis=-1, keepdims=True) + NORM_EPS) * g_ref[...]
    p = _dot(hn.astype(BF16), w_ref[...])
    pr = p[:, :n_shift]
    rows = pr.shape[0]
    rolled = pltpu.roll(pr, 1, 0)
    first = lax.broadcasted_iota(jnp.int32, pr.shape, 0) == 0
    prev = jnp.where(first, carry_ref[0:1, :], rolled)
    carry_ref[0:1, :] = pr[rows - 1:rows, :]
    s = pr + mu_ref[...] * (prev - pr)

    r = s[:, 0:RWKV_W]
    k = s[:, RWKV_W:2 * RWKV_W]
    v = s[:, 2 * RWKV_W:3 * RWKV_W]
    base = 3 * RWKV_W
    w_dn = s[:, base:base + LOW_SLOT]
    a_dn = s[:, base + LOW_SLOT:base + 2 * LOW_SLOT]
    g_dn = s[:, base + 2 * LOW_SLOT:base + 3 * LOW_SLOT]
    logw = -DECAY_SCALE * jax.nn.sigmoid(
        decay0_ref[...] + _dot(jnp.tanh(w_dn).astype(BF16), decay_up_ref[...]))
    iclr = jax.nn.sigmoid(iclr0_ref[...] + _dot(a_dn.astype(BF16), iclr_up_ref[...]))
    gate = _dot(jax.nn.sigmoid(g_dn).astype(BF16), gate_up_ref[...])
    if has_vres:
        vr_dn = s[:, base + 3 * LOW_SLOT:base + 4 * LOW_SLOT]
        mix = jax.nn.sigmoid(vres0_ref[...] + _dot(vr_dn.astype(BF16), vres_up_ref[...]))
        v = v + (vfirst_ref[0] - v) * mix

    ones_bd = _head_ones(RWKV_W)
    kk = k * kk_g_ref[...]
    kk = kk * lax.rsqrt(jnp.maximum(_head_sum(kk * kk, ones_bd), 1e-12))
    k = k * (1.0 + (iclr - 1.0) * ka_ref[...])
    bonus = _head_sum(r * k * rk_ref[...], ones_bd) * v

    r_o[0] = r
    lw_o[0] = logw
    k_o[0] = k
    v_o[0] = v
    kk_o[0] = kk
    a_o[0] = iclr
    gate_o[0] = gate
    bonus_o[0] = bonus
    q_o[0] = (p[:, n_shift:n_shift + SB_W] * (HEAD_DIM ** -0.5)).astype(BF16)
    sk_o[0] = p[:, n_shift + SB_W:n_shift + 2 * SB_W].astype(BF16)
    sv_o[0] = p[:, n_shift + 2 * SB_W:n_shift + 3 * SB_W].astype(BF16)


def _pad_cols(a, width):
    return jnp.pad(a, ((0, 0), (0, width - a.shape[1])))


def _pad_rows(a, height):
    return jnp.pad(a, ((0, height - a.shape[0]), (0, 0)))


def _mix_in(x, p, v_first, tile):
    B, T, D = x.shape
    has_vres = v_first is not None
    w_in, mu = p['w_in'], p['mu']
    ranks = [32, 32, 96] + ([32] if has_vres else [])
    off = 3 * RWKV_W
    low_w, low_mu = [], []
    for rk in ranks:
        low_w.append(_pad_cols(w_in[:, off:off + rk], LOW_SLOT))
        low_mu.append(jnp.pad(mu[off:off + rk], (0, LOW_SLOT - rk)))
        off += rk
    if not has_vres:
        low_w.append(jnp.zeros((D, LOW_SLOT), F32))
        low_mu.append(jnp.zeros((LOW_SLOT,), F32))
    w_all = jnp.concatenate([w_in[:, :3 * RWKV_W]] + low_w + [w_in[:, off:]], axis=1).astype(BF16)
    mu_all = jnp.concatenate([mu[:3 * RWKV_W]] + low_mu)[None, :]
    n_shift = 3 * RWKV_W + 4 * LOW_SLOT
    n_cols = n_shift + 3 * SB_W
    assert w_all.shape == (D, n_cols)

    row = lambda a: a.reshape(1, -1).astype(F32)
    up = lambda a: _pad_rows(a, LOW_SLOT).astype(BF16)
    if has_vres:
        vres0, vres_up, vf = row(p['vres0']), up(p['vres_up']), v_first
    else:
        vres0 = jnp.zeros((1, RWKV_W), F32)
        vres_up = jnp.zeros((LOW_SLOT, RWKV_W), BF16)
        vf = jnp.zeros((B, 8, RWKV_W), F32)

    grid = (B, T // tile)
    full = lambda shape: pl.BlockSpec(shape, lambda b, t: (0,) * len(shape))
    tok = lambda width: pl.BlockSpec((1, tile, width), lambda b, t: (b, t, 0))
    vf_spec = tok(RWKV_W) if has_vres else pl.BlockSpec((1, 8, RWKV_W), lambda b, t: (b, 0, 0))
    f32_out = jax.ShapeDtypeStruct((B, T, RWKV_W), F32)
    bf_out = jax.ShapeDtypeStruct((B, T, SB_W), BF16)
    outs = pl.pallas_call(
        functools.partial(_mix_in_kernel, has_vres),
        grid=grid,
        in_specs=[tok(D), full((1, D)), full((D, n_cols)), full((1, n_shift)),
                  full((1, RWKV_W)), full((LOW_SLOT, RWKV_W)),
                  full((1, RWKV_W)), full((LOW_SLOT, RWKV_W)),
                  full((LOW_SLOT, RWKV_W)),
                  full((1, RWKV_W)), full((LOW_SLOT, RWKV_W)),
                  full((1, RWKV_W)), full((1, RWKV_W)), full((1, RWKV_W)),
                  vf_spec],
        out_specs=[tok(RWKV_W)] * 8 + [tok(SB_W)] * 3,
        out_shape=[f32_out] * 8 + [bf_out] * 3,
        scratch_shapes=[pltpu.VMEM((8, n_shift), F32)],
        compiler_params=pltpu.CompilerParams(
            dimension_semantics=("arbitrary", "arbitrary"), vmem_limit_bytes=VMEM_LIMIT),
        name="mix_in",
    )(x, row(p['mix_pre']), w_all, mu_all,
      row(p['decay0']), up(p['decay_up']), row(p['iclr0']), up(p['iclr_up']), up(p['gate_up']),
      vres0, vres_up, row(p['k_k']), row(p['k_a']), row(p['r_k']), vf)
    return outs


def _stack(q, m0, m1):
    return jnp.concatenate([q * m0, q * m1], axis=0)


def _rwkv_kernel(n_chunks, r_ref, lw_ref, k_ref, v_ref, kk_ref, a_ref, y_ref, s_ref):
    t = pl.program_id(2)

    @pl.when(t == 0)
    def _():
        s_ref[...] = jnp.zeros_like(s_ref)

    C = CHUNK
    C2 = 2 * C
    lane = lax.broadcasted_iota(jnp.int32, (1, LANES), 1)
    m0 = jnp.where(lane < HEAD_DIM, 1.0, 0.0)
    m1 = 1.0 - m0
    ri = lax.broadcasted_iota(jnp.int32, (C, C), 0)
    ci = lax.broadcasted_iota(jnp.int32, (C, C), 1)
    tri_incl = jnp.where(ci <= ri, 1.0, 0.0).astype(BF16)
    r2 = lax.broadcasted_iota(jnp.int32, (C2, C2), 0)
    c2 = lax.broadcasted_iota(jnp.int32, (C2, C2), 1)
    same_head = (r2 // C) == (c2 // C)
    strict = jnp.logical_and(same_head, c2 < r2)
    incl = jnp.logical_and(same_head, c2 <= r2)
    eye = jnp.where(r2 == c2, 1.0, 0.0)

    def bf(a):
        return a.astype(BF16)

    for c in range(n_chunks):
        sl = pl.ds(c * C, C)
        r = r_ref[0, sl, :]
        lw = lw_ref[0, sl, :]
        k = k_ref[0, sl, :]
        v = v_ref[0, sl, :]
        kk = kk_ref[0, sl, :]
        a = a_ref[0, sl, :]

        lw_hi = bf(lw)
        lw_mid = bf(lw - lw_hi.astype(F32))
        lw_lo = bf(lw - lw_hi.astype(F32) - lw_mid.astype(F32))
        cum = _dot(tri_incl, lw_hi) + (_dot(tri_incl, lw_mid) + _dot(tri_incl, lw_lo))
        cum_prev = cum - lw
        total = cum[C - 1:C, :]
        e_pos = jnp.exp(cum)
        e_neg = jnp.exp(-cum)
        e_tail = jnp.exp(total - cum)
        b = kk * a
        rt = _stack(r * e_pos, m0, m1)
        zt = _stack(-kk * jnp.exp(cum_prev), m0, m1)
        bt = _stack(b * e_neg, m0, m1)
        kt = _stack(k * e_neg, m0, m1)
        bh = _stack(b * e_tail, m0, m1)
        kh = _stack(k * e_tail, m0, m1)
        vs = _stack(v, m0, m1)

        lhs = bf(jnp.concatenate([zt, rt], axis=0))
        rhs = bf(jnp.concatenate([bt, kt], axis=0))
        g = _dot_nt(lhs, rhs)
        l_zb = jnp.where(strict, g[:C2, :C2], 0.0)
        l_zk = jnp.where(strict, g[:C2, C2:], 0.0)
        l_rb = jnp.where(incl, g[C2:, :C2], 0.0)
        l_rk = jnp.where(incl, g[C2:, C2:], 0.0)

        inv = eye + l_zb
        pw = l_zb
        steps = int(math.log2(C)) - 1
        for _ in range(steps):
            pw = _dot(bf(pw), bf(pw))
            inv = inv + _dot(bf(inv), bf(pw))

        vs_b = bf(vs)
        zk_v = _dot(bf(l_zk), vs_b)
        zp = _dot(bf(inv), bf(jnp.concatenate([zt, zk_v], axis=1)))
        z_hat = zp[:, :LANES]
        p_st = zp[:, LANES:]
        rk_v = _dot(bf(l_rk), vs_b)
        ry = _dot(bf(l_rb), bf(zp))
        r_hat = rt + ry[:, :LANES]
        y_in = rk_v + ry[:, LANES:]
        a_mat = _dot_tn(bf(z_hat), bf(bh))
        d_mat = _dot_tn(bf(jnp.concatenate([p_st, vs], axis=0)),
                        bf(jnp.concatenate([bh, kh], axis=0)))

        s = s_ref[...]
        s_b = bf(s)
        y_st = _dot_nt(bf(r_hat), s_b) + y_in
        y_ref[0, sl, :] = y_st[:C, :] + y_st[C:, :]
        s_ref[...] = s * jnp.exp(total) + _dot(s_b, bf(a_mat)) + d_mat


def _rwkv_scan(r, lw, k, v, kk, a, tile):
    B, T, W = r.shape
    n_pairs = W // LANES
    spec = pl.BlockSpec((1, tile, LANES), lambda b, h, t: (b, t, h))
    return pl.pallas_call(
        functools.partial(_rwkv_kernel, tile // CHUNK),
        grid=(B, n_pairs, T // tile),
        in_specs=[spec] * 6,
        out_specs=spec,
        out_shape=jax.ShapeDtypeStruct((B, T, W), F32),
        scratch_shapes=[pltpu.VMEM((LANES, LANES), F32)],
        compiler_params=pltpu.CompilerParams(
            dimension_semantics=("arbitrary", "arbitrary", "arbitrary"),
            vmem_limit_bytes=VMEM_LIMIT),
        name="rwkv_scan",
    )(r, lw, k, v, kk, a)


SB_LOG_ZERO = -104.0


def _sb_kernel(blk, q_ref, k_ref, v_ref, o_ref, acc_ref, carry_ref):
    qi = pl.program_id(2)
    lane = lax.broadcasted_iota(jnp.int32, (1, LANES), 1)
    in_h0 = lane < HEAD_DIM
    q = q_ref[0]
    zero = jnp.zeros_like(q)
    q_heads = (jnp.where(in_h0, q, zero), jnp.where(in_h0, zero, q))
    row = lax.broadcasted_iota(jnp.int32, (blk, blk), 0)
    col = lax.broadcasted_iota(jnp.int32, (blk, blk), 1)
    causal = col < row
    tri = jnp.where(row > col, 1.0, 0.0).astype(BF16)

    def one_block(j, h, masked):
        start = pl.multiple_of(j * blk, blk)
        kj = k_ref[0, pl.ds(start, blk), :]
        vj = v_ref[0, pl.ds(start, blk), :]
        z = _dot_nt(q_heads[h], kj)
        sp = jnp.maximum(z, 0.0) + jnp.log(1.0 + jnp.exp(-jnp.abs(z)))
        lk = -sp
        if masked:
            lk = jnp.where(causal, lk, 0.0)
        hi = lk.astype(BF16)
        lo = (lk - hi.astype(F32)).astype(BF16)
        btw = _dot(hi, tri) + _dot(lo, tri)
        carry = carry_ref[h]
        att = jnp.exp(z - sp + btw + carry[:, 0:1])
        if masked:
            att = jnp.where(causal, att, 0.0)
        acc_ref[h] += _dot(att.astype(BF16), vj)
        new_carry = carry + (btw[:, 0:1] + lk[:, 0:1])
        carry_ref[h] = new_carry
        return jnp.max(new_carry)

    acc_ref[...] = jnp.zeros_like(acc_ref)
    carry_ref[...] = jnp.zeros_like(carry_ref)
    top = jnp.maximum(one_block(qi, 0, True), one_block(qi, 1, True))

    def cond(state):
        j, top = state
        return jnp.logical_and(j >= 0, top > SB_LOG_ZERO)

    def body(state):
        j, _ = state
        top = jnp.maximum(one_block(j, 0, False), one_block(j, 1, False))
        return j - 1, top

    lax.while_loop(cond, body, (qi - 1, top))
    o_ref[0] = jnp.where(in_h0, acc_ref[0], acc_ref[1])


def _sb_attention(q, k, v, blk):
    B, T, W = q.shape
    n_pairs = W // LANES
    q_spec = pl.BlockSpec((1, blk, LANES), lambda b, h, t: (b, t, h))
    kv_spec = pl.BlockSpec((1, T, LANES), lambda b, h, t: (b, 0, h))
    return pl.pallas_call(
        functools.partial(_sb_kernel, blk),
        grid=(B, n_pairs, T // blk),
        in_specs=[q_spec, kv_spec, kv_spec],
        out_specs=q_spec,
        out_shape=jax.ShapeDtypeStruct((B, T, W), F32),
        scratch_shapes=[pltpu.VMEM((2, blk, LANES), F32), pltpu.VMEM((2, blk, LANES), F32)],
        compiler_params=pltpu.CompilerParams(
            dimension_semantics=("arbitrary", "arbitrary", "arbitrary"),
            vmem_limit_bytes=VMEM_LIMIT),
        name="sb_attention",
    )(q, k, v)


def _rms(x, g):
    return x * lax.rsqrt(jnp.mean(x * x, axis=-1, keepdims=True) + NORM_EPS) * g


def _split2(x):
    hi = x.astype(BF16)
    return hi, (x - hi.astype(F32)).astype(BF16)


def _mix_out_kernel(has_router, y_ref, bonus_ref, gate_ref, o_ref, x_ref,
                    lnx_g_ref, lnx_b_ref, sb_g_ref, w_out_ref, post_ref, pre_ref, router_ref,
                    h_o, hn_o, dg_o):
    ones_bd = _head_ones(RWKV_W)
    inv_n = 1.0 / HEAD_DIM
    y = y_ref[0]
    mean = _head_sum(y, ones_bd) * inv_n
    yc = y - mean
    var = _head_sum(yc * yc, ones_bd) * inv_n
    ya = (yc * lax.rsqrt(var + LNX_EPS) * lnx_g_ref[...] + lnx_b_ref[...] + bonus_ref[0]) * gate_ref[0]
    o = o_ref[0]
    yb = o * lax.rsqrt(_head_sum(o * o, ones_bd) * inv_n + NORM_EPS) * sb_g_ref[...]
    cat = jnp.concatenate([ya, yb], axis=-1).astype(BF16)
    y2 = _dot(cat, w_out_ref[...])
    h = x_ref[0] + _rms(y2, post_ref[...])
    hn = _rms(h, pre_ref[...])
    h_o[0] = h
    hn_o[0] = hn.astype(BF16)
    if has_router:
        hi, lo = _split2(hn)
        r_hi, r_lo = _split2(router_ref[...])
        logits = _dot(hi, r_hi) + (_dot(lo, r_hi) + _dot(hi, r_lo))
        lane = lax.broadcasted_iota(jnp.int32, logits.shape, 1)
        neg = jnp.float32(-jnp.inf)
        logits = jnp.where(lane < N_EXPERTS, logits, neg)
        m1 = jnp.max(logits, axis=-1, keepdims=True)
        i1 = jnp.min(jnp.where(logits == m1, lane, LANES), axis=-1, keepdims=True)
        rest = jnp.where(lane == i1, neg, logits)
        m2 = jnp.max(rest, axis=-1, keepdims=True)
        i2 = jnp.min(jnp.where(rest == m2, lane, LANES), axis=-1, keepdims=True)
        e2 = jnp.exp(m2 - m1)
        g1 = 1.0 / (1.0 + e2)
        dg_o[0] = jnp.where(lane == i1, g1, jnp.where(lane == i2, e2 * g1, 0.0))
    else:
        dg_o[0] = jnp.zeros_like(dg_o[0])


def _mix_out(y, bonus, gate, o, x, p, tile):
    B, T, D = x.shape
    has_router = 'router' in p
    row = lambda a: a.reshape(1, -1).astype(F32)
    router = _pad_cols(p['router'], LANES) if has_router else jnp.zeros((D, LANES), F32)
    full = lambda shape: pl.BlockSpec(shape, lambda b, t: (0,) * len(shape))
    tok = lambda width: pl.BlockSpec((1, tile, width), lambda b, t: (b, t, 0))
    return pl.pallas_call(
        functools.partial(_mix_out_kernel, has_router),
        grid=(B, T // tile),
        in_specs=[tok(RWKV_W)] * 4 + [tok(D)] +
                 [full((1, RWKV_W))] * 3 + [full((D, D)), full((1, D)), full((1, D)), full((D, LANES))],
        out_specs=[tok(D), tok(D), tok(LANES)],
        out_shape=[jax.ShapeDtypeStruct((B, T, D), F32), jax.ShapeDtypeStruct((B, T, D), BF16),
                   jax.ShapeDtypeStruct((B, T, LANES), F32)],
        compiler_params=pltpu.CompilerParams(
            dimension_semantics=("arbitrary", "arbitrary"), vmem_limit_bytes=VMEM_LIMIT),
        name="mix_out",
    )(y, bonus, gate, o, x, row(p['lnx_g']), row(p['lnx_b']), row(p['sb_g']),
      p['w_out'].astype(BF16), row(p['mix_post']), row(p['ffn_pre']), router)


def _ffn_kernel(h_ref, hn_ref, wg_ref, wu_ref, wd_ref, post_ref, o_ref, acc_ref):
    f = pl.program_id(1)

    @pl.when(f == 0)
    def _():
        acc_ref[...] = jnp.zeros_like(acc_ref)

    hn = hn_ref[...]
    g = _dot(hn, wg_ref[...])
    u = _dot(hn, wu_ref[...])
    act = (g * jax.nn.sigmoid(g) * u).astype(BF16)
    acc_ref[...] += _dot(act, wd_ref[...])

    @pl.when(f == pl.num_programs(1) - 1)
    def _():
        o_ref[...] = h_ref[...] + _rms(acc_ref[...], post_ref[...])


def _ffn(h, hn, w_gate, w_up, w_down, post, tile_m, tile_f):
    N, D = h.shape
    F = w_gate.shape[1]
    return pl.pallas_call(
        _ffn_kernel,
        grid=(N // tile_m, F // tile_f),
        in_specs=[pl.BlockSpec((tile_m, D), lambda i, f: (i, 0)),
                  pl.BlockSpec((tile_m, D), lambda i, f: (i, 0)),
                  pl.BlockSpec((D, tile_f), lambda i, f: (0, f)),
                  pl.BlockSpec((D, tile_f), lambda i, f: (0, f)),
                  pl.BlockSpec((tile_f, D), lambda i, f: (f, 0)),
                  pl.BlockSpec((1, D), lambda i, f: (0, 0))],
        out_specs=pl.BlockSpec((tile_m, D), lambda i, f: (i, 0)),
        out_shape=jax.ShapeDtypeStruct((N, D), F32),
        scratch_shapes=[pltpu.VMEM((tile_m, D), F32)],
        compiler_params=pltpu.CompilerParams(
            dimension_semantics=("arbitrary", "arbitrary"), vmem_limit_bytes=VMEM_LIMIT),
        name="ffn",
    )(h, hn, w_gate.astype(BF16), w_up.astype(BF16), w_down.astype(BF16),
      post.reshape(1, -1).astype(F32))


def _moe_kernel(h_ref, hn_ref, dg_ref, wg_ref, wu_ref, wd_ref, post_ref, o_ref, acc_ref):
    e = pl.program_id(1)
    f = pl.program_id(2)

    @pl.when(jnp.logical_and(e == 0, f == 0))
    def _():
        acc_ref[...] = jnp.zeros_like(acc_ref)

    hn = hn_ref[...]
    dg = dg_ref[...]
    lane = lax.broadcasted_iota(jnp.int32, dg.shape, 1)
    gate_e = jnp.sum(jnp.where(lane == e, dg, 0.0), axis=-1, keepdims=True)
    g = _dot(hn, wg_ref[0])
    u = _dot(hn, wu_ref[0])
    act = (g * jax.nn.sigmoid(g) * u * gate_e).astype(BF16)
    acc_ref[...] += _dot(act, wd_ref[0])

    @pl.when(jnp.logical_and(e == pl.num_programs(1) - 1, f == pl.num_programs(2) - 1))
    def _():
        o_ref[...] = h_ref[...] + _rms(acc_ref[...], post_ref[...])


def _moe(h, hn, dg, w_gate, w_up, w_down, post, tile_m, tile_f):
    N, D = h.shape
    E, _, F = w_gate.shape
    return pl.pallas_call(
        _moe_kernel,
        grid=(N // tile_m, E, F // tile_f),
        in_specs=[pl.BlockSpec((tile_m, D), lambda i, e, f: (i, 0)),
                  pl.BlockSpec((tile_m, D), lambda i, e, f: (i, 0)),
                  pl.BlockSpec((tile_m, LANES), lambda i, e, f: (i, 0)),
                  pl.BlockSpec((1, D, tile_f), lambda i, e, f: (e, 0, f)),
                  pl.BlockSpec((1, D, tile_f), lambda i, e, f: (e, 0, f)),
                  pl.BlockSpec((1, tile_f, D), lambda i, e, f: (e, f, 0)),
                  pl.BlockSpec((1, D), lambda i, e, f: (0, 0))],
        out_specs=pl.BlockSpec((tile_m, D), lambda i, e, f: (i, 0)),
        out_shape=jax.ShapeDtypeStruct((N, D), F32),
        scratch_shapes=[pltpu.VMEM((tile_m, D), F32)],
        compiler_params=pltpu.CompilerParams(
            dimension_semantics=("arbitrary", "arbitrary", "arbitrary"),
            vmem_limit_bytes=VMEM_LIMIT),
        name="moe",
    )(h, hn, dg, w_gate.astype(BF16), w_up.astype(BF16), w_down.astype(BF16),
      post.reshape(1, -1).astype(F32))


MIX_TILE = 256
RWKV_TILE = 256
SB_BLOCK = 256
FFN_TILE_M = 512
FFN_TILE_F = 512


def _layer(x, p, v_first):
    B, T, D = x.shape
    r, lw, k, v, kk, a, gate, bonus, q, sk, sv = _mix_in(x, p, v_first, MIX_TILE)
    y = _rwkv_scan(r, lw, k, v, kk, a, RWKV_TILE)
    o = _sb_attention(q, sk, sv, SB_BLOCK)
    h, hn, dg = _mix_out(y, bonus, gate, o, x, p, MIX_TILE)
    h2, hn2 = h.reshape(B * T, D), hn.reshape(B * T, D)
    if 'router' in p:
        out = _moe(h2, hn2, dg.reshape(B * T, LANES), p['moe_gate'], p['moe_up'], p['moe_down'],
                   p['ffn_post'], FFN_TILE_M, FFN_TILE_F)
    else:
        out = _ffn(h2, hn2, p['ff_gate'], p['ff_up'], p['ff_down'], p['ffn_post'],
                   FFN_TILE_M, FFN_TILE_F)
    return out.reshape(B, T, D), (v if v_first is None else v_first)


def kernel(x, mix_pre_0, w_in_0, mu_0, decay0_0, decay_up_0, iclr0_0, iclr_up_0, gate_up_0, k_k_0, k_a_0, r_k_0, lnx_g_0, lnx_b_0, sb_g_0, w_out_0, mix_post_0, ffn_pre_0, ff_gate_0, ff_up_0, ff_down_0, ffn_post_0, mix_pre_1, w_in_1, mu_1, decay0_1, decay_up_1, iclr0_1, iclr_up_1, gate_up_1, vres0_1, vres_up_1, k_k_1, k_a_1, r_k_1, lnx_g_1, lnx_b_1, sb_g_1, w_out_1, mix_post_1, ffn_pre_1, router_1, moe_gate_1, moe_up_1, moe_down_1, ffn_post_1):
    p0 = dict(mix_pre=mix_pre_0, w_in=w_in_0, mu=mu_0, decay0=decay0_0, decay_up=decay_up_0,
              iclr0=iclr0_0, iclr_up=iclr_up_0, gate_up=gate_up_0, k_k=k_k_0, k_a=k_a_0,
              r_k=r_k_0, lnx_g=lnx_g_0, lnx_b=lnx_b_0, sb_g=sb_g_0, w_out=w_out_0,
              mix_post=mix_post_0, ffn_pre=ffn_pre_0, ff_gate=ff_gate_0, ff_up=ff_up_0,
              ff_down=ff_down_0, ffn_post=ffn_post_0)
    p1 = dict(mix_pre=mix_pre_1, w_in=w_in_1, mu=mu_1, decay0=decay0_1, decay_up=decay_up_1,
              iclr0=iclr0_1, iclr_up=iclr_up_1, gate_up=gate_up_1, vres0=vres0_1,
              vres_up=vres_up_1, k_k=k_k_1, k_a=k_a_1, r_k=r_k_1, lnx_g=lnx_g_1, lnx_b=lnx_b_1,
              sb_g=sb_g_1, w_out=w_out_1, mix_post=mix_post_1, ffn_pre=ffn_pre_1,
              router=router_1, moe_gate=moe_gate_1, moe_up=moe_up_1, moe_down=moe_down_1,
              ffn_post=ffn_post_1)
    x, v_first = _layer(x, p0, None)
    x, _ = _layer(x, p1, v_first)
    return x
```

```python
import functools
import math

import jax
import jax.numpy as jnp
from jax import lax
from jax.experimental import pallas as pl
from jax.experimental.pallas import tpu as pltpu

HEAD_DIM = 64
RWKV_W = 512
SB_W = 512
N_EXPERTS = 8
NORM_EPS = 1e-6
LNX_EPS = 64e-5
DECAY_SCALE = math.exp(-0.5)
LANES = 128
CHUNK = 64
LOW_SLOT = 128
VMEM_LIMIT = 56 * 1024 * 1024

F32 = jnp.float32
BF16 = jnp.bfloat16


def _dot(a, b):
    return jnp.dot(a, b, preferred_element_type=F32)


def _dot_nt(a, b):
    return lax.dot_general(a, b, (((1,), (1,)), ((), ())), preferred_element_type=F32)


def _dot_tn(a, b):
    return lax.dot_general(a, b, (((0,), (0,)), ((), ())), preferred_element_type=F32)


def _head_ones(width):
    r = lax.broadcasted_iota(jnp.int32, (width, width), 0) // HEAD_DIM
    c = lax.broadcasted_iota(jnp.int32, (width, width), 1) // HEAD_DIM
    return jnp.where(r == c, 1.0, 0.0).astype(BF16)


def _head_sum(x, ones_bd):
    hi = x.astype(BF16)
    lo = (x - hi.astype(F32)).astype(BF16)
    return _dot(hi, ones_bd) + _dot(lo, ones_bd)


def _mix_in_kernel(has_vres, x_ref, g_ref, w_ref, mu_ref, decay0_ref, decay_up_ref,
                   iclr0_ref, iclr_up_ref, gate_up_ref, vres0_ref, vres_up_ref,
                   kk_g_ref, ka_ref, rk_ref, vfirst_ref,
                   r_o, lw_o, k_o, v_o, kk_o, a_o, gate_o, bonus_o, q_o, sk_o, sv_o,
                   carry_ref):
    t = pl.program_id(1)
    n_shift = 3 * RWKV_W + 4 * LOW_SLOT

    @pl.when(t == 0)
    def _():
        carry_ref[...] = jnp.zeros_like(carry_ref)

    x = x_ref[0]
    hn = x * lax.rsqrt(jnp.mean(x * x, axis=-1, keepdims=True) + NORM_EPS) * g_ref[...]
    p = _dot(hn.astype(BF16), w_ref[...])
    pr = p[:, :n_shift]
    rows = pr.shape[0]
    rolled = pltpu.roll(pr, 1, 0)
    first = lax.broadcasted_iota(jnp.int32, pr.shape, 0) == 0
    prev = jnp.where(first, carry_ref[0:1, :], rolled)
    carry_ref[0:1, :] = pr[rows - 1:rows, :]
    s = pr + mu_ref[...] * (prev - pr)

    r = s[:, 0:RWKV_W]
    k = s[:, RWKV_W:2 * RWKV_W]
    v = s[:, 2 * RWKV_W:3 * RWKV_W]
    base = 3 * RWKV_W
    w_dn = s[:, base:base + LOW_SLOT]
    a_dn = s[:, base + LOW_SLOT:base + 2 * LOW_SLOT]
    g_dn = s[:, base + 2 * LOW_SLOT:base + 3 * LOW_SLOT]
    logw = -DECAY_SCALE * jax.nn.sigmoid(
        decay0_ref[...] + _dot(jnp.tanh(w_dn).astype(BF16), decay_up_ref[...]))
    iclr = jax.nn.sigmoid(iclr0_ref[...] + _dot(a_dn.astype(BF16), iclr_up_ref[...]))
    gate = _dot(jax.nn.sigmoid(g_dn).astype(BF16), gate_up_ref[...])
    if has_vres:
        vr_dn = s[:, base + 3 * LOW_SLOT:base + 4 * LOW_SLOT]
        mix = jax.nn.sigmoid(vres0_ref[...] + _dot(vr_dn.astype(BF16), vres_up_ref[...]))
        v = v + (vfirst_ref[0] - v) * mix

    ones_bd = _head_ones(RWKV_W)
    kk = k * kk_g_ref[...]
    kk = kk * lax.rsqrt(jnp.maximum(_head_sum(kk * kk, ones_bd), 1e-12))
    k = k * (1.0 + (iclr - 1.0) * ka_ref[...])
    bonus = _head_sum(r * k * rk_ref[...], ones_bd) * v

    r_o[0] = r
    lw_o[0] = logw
    k_o[0] = k
    v_o[0] = v
    kk_o[0] = kk
    a_o[0] = iclr
    gate_o[0] = gate
    bonus_o[0] = bonus
    q_o[0] = (p[:, n_shift:n_shift + SB_W] * (HEAD_DIM ** -0.5)).astype(BF16)
    sk_o[0] = p[:, n_shift + SB_W:n_shift + 2 * SB_W].astype(BF16)
    sv_o[0] = p[:, n_shift + 2 * SB_W:n_shift + 3 * SB_W].astype(BF16)


def _pad_cols(a, width):
    return jnp.pad(a, ((0, 0), (0, width - a.shape[1])))


def _pad_rows(a, height):
    return jnp.pad(a, ((0, height - a.shape[0]), (0, 0)))


def _mix_in(x, p, v_first, tile):
    B, T, D = x.shape
    has_vres = v_first is not None
    w_in, mu = p['w_in'], p['mu']
    ranks = [32, 32, 96] + ([32] if has_vres else [])
    off = 3 * RWKV_W
    low_w, low_mu = [], []
    for rk in ranks:
        low_w.append(_pad_cols(w_in[:, off:off + rk], LOW_SLOT))
        low_mu.append(jnp.pad(mu[off:off + rk], (0, LOW_SLOT - rk)))
        off += rk
    if not has_vres:
        low_w.append(jnp.zeros((D, LOW_SLOT), F32))
        low_mu.append(jnp.zeros((LOW_SLOT,), F32))
    w_all = jnp.concatenate([w_in[:, :3 * RWKV_W]] + low_w + [w_in[:, off:]], axis=1).astype(BF16)
    mu_all = jnp.concatenate([mu[:3 * RWKV_W]] + low_mu)[None, :]
    n_shift = 3 * RWKV_W + 4 * LOW_SLOT
    n_cols = n_shift + 3 * SB_W
    assert w_all.shape == (D, n_cols)

    row = lambda a: a.reshape(1, -1).astype(F32)
    up = lambda a: _pad_rows(a, LOW_SLOT).astype(BF16)
    if has_vres:
        vres0, vres_up, vf = row(p['vres0']), up(p['vres_up']), v_first
    else:
        vres0 = jnp.zeros((1, RWKV_W), F32)
        vres_up = jnp.zeros((LOW_SLOT, RWKV_W), BF16)
        vf = jnp.zeros((B, 8, RWKV_W), F32)

    grid = (B, T // tile)
    full = lambda shape: pl.BlockSpec(shape, lambda b, t: (0,) * len(shape))
    tok = lambda width: pl.BlockSpec((1, tile, width), lambda b, t: (b, t, 0))
    vf_spec = tok(RWKV_W) if has_vres else pl.BlockSpec((1, 8, RWKV_W), lambda b, t: (b, 0, 0))
    f32_out = jax.ShapeDtypeStruct((B, T, RWKV_W), F32)
    bf_out = jax.ShapeDtypeStruct((B, T, SB_W), BF16)
    outs = pl.pallas_call(
        functools.partial(_mix_in_kernel, has_vres),
        grid=grid,
        in_specs=[tok(D), full((1, D)), full((D, n_cols)), full((1, n_shift)),
                  full((1, RWKV_W)), full((LOW_SLOT, RWKV_W)),
                  full((1, RWKV_W)), full((LOW_SLOT, RWKV_W)),
                  full((LOW_SLOT, RWKV_W)),
                  full((1, RWKV_W)), full((LOW_SLOT, RWKV_W)),
                  full((1, RWKV_W)), full((1, RWKV_W)), full((1, RWKV_W)),
                  vf_spec],
        out_specs=[tok(RWKV_W)] * 8 + [tok(SB_W)] * 3,
        out_shape=[f32_out] * 8 + [bf_out] * 3,
        scratch_shapes=[pltpu.VMEM((8, n_shift), F32)],
        compiler_params=pltpu.CompilerParams(
            dimension_semantics=("arbitrary", "arbitrary"), vmem_limit_bytes=VMEM_LIMIT),
        name="mix_in",
    )(x, row(p['mix_pre']), w_all, mu_all,
      row(p['decay0']), up(p['decay_up']), row(p['iclr0']), up(p['iclr_up']), up(p['gate_up']),
      vres0, vres_up, row(p['k_k']), row(p['k_a']), row(p['r_k']), vf)
    return outs


def _stack(q, m0, m1):
    return jnp.concatenate([q * m0, q * m1], axis=0)


def _rwkv_kernel(n_chunks, r_ref, lw_ref, k_ref, v_ref, kk_ref, a_ref, y_ref,
                 s_ref, rhat_sc, yin_sc, amat_sc, dmat_sc, decay_sc):
    t = pl.program_id(2)
    cur = lax.rem(t, 2)
    prev = 1 - cur
    C = CHUNK

    @pl.when(t == 0)
    def _():
        s_ref[...] = jnp.zeros_like(s_ref)
        rhat_sc[1] = jnp.zeros_like(rhat_sc[1])
        yin_sc[1] = jnp.zeros_like(yin_sc[1])
        amat_sc[1] = jnp.zeros_like(amat_sc[1])
        dmat_sc[1] = jnp.zeros_like(dmat_sc[1])
        decay_sc[1] = jnp.zeros_like(decay_sc[1])

    scan = {'s': s_ref[...], 'done': 0}
    n_slots = 8

    def scan_steps(slot):
        upto = ((slot + 1) * n_chunks) // n_slots
        for c in range(scan['done'], upto):
            s = scan['s']
            s_b = s.astype(BF16)
            y_st = _dot_nt(rhat_sc[prev, c], s_b)
            y_ref[0, c * C:(c + 1) * C, :] = y_st[:C, :] + y_st[C:, :] + yin_sc[prev, c]
            scan['s'] = (s * decay_sc[prev, c][0:1, :] + _dot(s_b, amat_sc[prev, c])
                         + dmat_sc[prev, c])
        scan['done'] = upto

    C2 = 2 * C
    lane = lax.broadcasted_iota(jnp.int32, (1, LANES), 1)
    m0 = jnp.where(lane < HEAD_DIM, 1.0, 0.0)
    m1 = 1.0 - m0
    ti = lax.broadcasted_iota(jnp.int32, (C, C), 0)
    tj = lax.broadcasted_iota(jnp.int32, (C, C), 1)
    tri = jnp.where(tj <= ti, 1.0, 0.0).astype(BF16)
    r2 = lax.broadcasted_iota(jnp.int32, (C2, C2), 0)
    c2 = lax.broadcasted_iota(jnp.int32, (C2, C2), 1)
    same_head = (r2 // C) == (c2 // C)
    strict = jnp.logical_and(same_head, c2 < r2)
    incl = jnp.logical_and(same_head, c2 <= r2)
    eye = jnp.where(r2 == c2, 1.0, 0.0)

    def bf(a):
        return a.astype(BF16)

    r = r_ref[0]
    lw = lw_ref[0]
    k = k_ref[0]
    v = v_ref[0]
    kk = kk_ref[0]
    b = kk * a_ref[0]

    lw_hi = bf(lw)
    lw_mid = bf(lw - lw_hi.astype(F32))
    lw_lo = bf(lw - lw_hi.astype(F32) - lw_mid.astype(F32))
    cums = []
    for c in range(n_chunks):
        sl = slice(c * C, (c + 1) * C)
        cums.append(_dot(tri, lw_hi[sl]) + (_dot(tri, lw_mid[sl]) + _dot(tri, lw_lo[sl])))
    cum = jnp.concatenate(cums, axis=0)
    total = jnp.concatenate(
        [jnp.broadcast_to(cc[C - 1:C, :], (C, LANES)) for cc in cums], axis=0)
    scan_steps(0)
    e_neg = jnp.exp(-cum)
    e_tail = jnp.exp(total - cum)
    rt_all = r * jnp.exp(cum)
    zt_all = -kk * jnp.exp(cum - lw)
    bt_all = b * e_neg
    kt_all = k * e_neg
    bh_all = b * e_tail
    kh_all = k * e_tail
    decay_all = jnp.exp(total)

    chunks = range(n_chunks)
    cut = lambda x, c: _stack(x[c * C:(c + 1) * C, :], m0, m1)
    rt = [cut(rt_all, c) for c in chunks]
    zt = [cut(zt_all, c) for c in chunks]
    bh = [bf(cut(bh_all, c)) for c in chunks]
    kh = [bf(cut(kh_all, c)) for c in chunks]
    vs = [bf(cut(v, c)) for c in chunks]
    g = [_dot_nt(bf(jnp.concatenate([zt[c], rt[c]], axis=0)),
                 bf(jnp.concatenate([cut(bt_all, c), cut(kt_all, c)], axis=0))) for c in chunks]
    scan_steps(1)
    l_zb = [jnp.where(strict, g[c][:C2, :C2], 0.0) for c in chunks]
    l_zk = [bf(jnp.where(strict, g[c][:C2, C2:], 0.0)) for c in chunks]
    l_rb = [bf(jnp.where(incl, g[c][C2:, :C2], 0.0)) for c in chunks]
    l_rk = [bf(jnp.where(incl, g[c][C2:, C2:], 0.0)) for c in chunks]

    inv = [eye + l_zb[c] for c in chunks]
    pw = l_zb
    for i in range(int(math.log2(C)) - 1):
        pw_b = [bf(p) for p in pw]
        pw = [_dot(p, p) for p in pw_b]
        inv = [inv[c] + _dot(bf(inv[c]), bf(pw[c])) for c in chunks]
        scan_steps(2 + i)

    zk_v = [_dot(l_zk[c], vs[c]) for c in chunks]
    zp = [_dot(bf(inv[c]), bf(jnp.concatenate([zt[c], zk_v[c]], axis=1))) for c in chunks]
    scan_steps(n_slots - 1)
    s_ref[...] = scan['s']
    zp_b = [bf(x) for x in zp]
    rk_v = [_dot(l_rk[c], vs[c]) for c in chunks]
    ry = [_dot(l_rb[c], zp_b[c]) for c in chunks]
    r_hat = [bf(rt[c] + ry[c][:, :LANES]) for c in chunks]
    y_in = [rk_v[c] + ry[c][:, LANES:] for c in chunks]
    a_mat = [bf(_dot_tn(zp_b[c][:, :LANES], bh[c])) for c in chunks]
    d_mat = [_dot_tn(jnp.concatenate([zp_b[c][:, LANES:], vs[c]], axis=0),
                     jnp.concatenate([bh[c], kh[c]], axis=0)) for c in chunks]

    for c in chunks:
        rhat_sc[cur, c] = r_hat[c]
        yin_sc[cur, c] = y_in[c][:C, :] + y_in[c][C:, :]
        amat_sc[cur, c] = a_mat[c]
        dmat_sc[cur, c] = d_mat[c]
        decay_sc[cur, c] = decay_all[c * C:c * C + 8, :]


def _rwkv_scan(r, lw, k, v, kk, a, tile):
    B, T, W = r.shape
    n_pairs = W // LANES
    n_tiles = T // tile
    n_chunks = tile // CHUNK
    in_spec = pl.BlockSpec((1, tile, LANES), lambda b, h, t: (b, jnp.minimum(t, n_tiles - 1), h))
    out_spec = pl.BlockSpec((1, tile, LANES), lambda b, h, t: (b, jnp.maximum(t - 1, 0), h))
    return pl.pallas_call(
        functools.partial(_rwkv_kernel, n_chunks),
        grid=(B, n_pairs, n_tiles + 1),
        in_specs=[in_spec] * 6,
        out_specs=out_spec,
        out_shape=jax.ShapeDtypeStruct((B, T, W), F32),
        scratch_shapes=[pltpu.VMEM((LANES, LANES), F32),
                        pltpu.VMEM((2, n_chunks, 2 * CHUNK, LANES), BF16),
                        pltpu.VMEM((2, n_chunks, CHUNK, LANES), F32),
                        pltpu.VMEM((2, n_chunks, LANES, LANES), BF16),
                        pltpu.VMEM((2, n_chunks, LANES, LANES), F32),
                        pltpu.VMEM((2, n_chunks, 8, LANES), F32)],
        compiler_params=pltpu.CompilerParams(
            dimension_semantics=("arbitrary", "arbitrary", "arbitrary"),
            vmem_limit_bytes=VMEM_LIMIT),
        name="rwkv_scan",
    )(r, lw, k, v, kk, a)


SB_LOG_ZERO = -104.0


def _sb_kernel(blk, q_ref, k_ref, v_ref, o_ref, acc_ref, carry_ref):
    qi = pl.program_id(2)
    lane = lax.broadcasted_iota(jnp.int32, (1, LANES), 1)
    in_h0 = lane < HEAD_DIM
    q = q_ref[0]
    zero = jnp.zeros_like(q)
    q_heads = (jnp.where(in_h0, q, zero), jnp.where(in_h0, zero, q))
    row = lax.broadcasted_iota(jnp.int32, (blk, blk), 0)
    col = lax.broadcasted_iota(jnp.int32, (blk, blk), 1)
    causal = col < row
    tri = jnp.where(row > col, 1.0, 0.0).astype(BF16)

    heads = (0, 1)

    def one_block(j, masked):
        start = pl.multiple_of(j * blk, blk)
        kj = k_ref[0, pl.ds(start, blk), :]
        vj = v_ref[0, pl.ds(start, blk), :]
        z = [_dot_nt(q_heads[h], kj) for h in heads]
        sp = [jnp.maximum(x, 0.0) + jnp.log(1.0 + jnp.exp(-jnp.abs(x))) for x in z]
        lk = [-x for x in sp]
        if masked:
            lk = [jnp.where(causal, x, 0.0) for x in lk]
        hi = [x.astype(BF16) for x in lk]
        lo = [(lk[h] - hi[h].astype(F32)).astype(BF16) for h in heads]
        btw = [_dot(hi[h], tri) + _dot(lo[h], tri) for h in heads]
        carry = [carry_ref[h] for h in heads]
        att = [jnp.exp(z[h] - sp[h] + btw[h] + carry[h][:, 0:1]) for h in heads]
        if masked:
            att = [jnp.where(causal, x, 0.0) for x in att]
        for h in heads:
            acc_ref[h] += _dot(att[h].astype(BF16), vj)
        new_carry = [carry[h] + (btw[h][:, 0:1] + lk[h][:, 0:1]) for h in heads]
        for h in heads:
            carry_ref[h] = new_carry[h]
        return jnp.max(jnp.maximum(new_carry[0], new_carry[1]))

    acc_ref[...] = jnp.zeros_like(acc_ref)
    carry_ref[...] = jnp.zeros_like(carry_ref)
    top = one_block(qi, True)

    def cond(state):
        j, top = state
        return jnp.logical_and(j >= 0, top > SB_LOG_ZERO)

    def body(state):
        j, _ = state
        return j - 1, one_block(j, False)

    lax.while_loop(cond, body, (qi - 1, top))
    o_ref[0] = jnp.where(in_h0, acc_ref[0], acc_ref[1])


def _sb_attention(q, k, v, blk):
    B, T, W = q.shape
    n_pairs = W // LANES
    q_spec = pl.BlockSpec((1, blk, LANES), lambda b, h, t: (b, t, h))
    kv_spec = pl.BlockSpec((1, T, LANES), lambda b, h, t: (b, 0, h))
    return pl.pallas_call(
        functools.partial(_sb_kernel, blk),
        grid=(B, n_pairs, T // blk),
        in_specs=[q_spec, kv_spec, kv_spec],
        out_specs=q_spec,
        out_shape=jax.ShapeDtypeStruct((B, T, W), F32),
        scratch_shapes=[pltpu.VMEM((2, blk, LANES), F32), pltpu.VMEM((2, blk, LANES), F32)],
        compiler_params=pltpu.CompilerParams(
            dimension_semantics=("arbitrary", "arbitrary", "arbitrary"),
            vmem_limit_bytes=VMEM_LIMIT),
        name="sb_attention",
    )(q, k, v)


def _rms(x, g):
    return x * lax.rsqrt(jnp.mean(x * x, axis=-1, keepdims=True) + NORM_EPS) * g


def _split2(x):
    hi = x.astype(BF16)
    return hi, (x - hi.astype(F32)).astype(BF16)


def _mix_out_kernel(has_router, y_ref, bonus_ref, gate_ref, o_ref, x_ref,
                    lnx_g_ref, lnx_b_ref, sb_g_ref, w_out_ref, post_ref, pre_ref, router_ref,
                    h_o, hn_o, dg_o):
    ones_bd = _head_ones(RWKV_W)
    inv_n = 1.0 / HEAD_DIM
    y = y_ref[0]
    mean = _head_sum(y, ones_bd) * inv_n
    yc = y - mean
    var = _head_sum(yc * yc, ones_bd) * inv_n
    ya = (yc * lax.rsqrt(var + LNX_EPS) * lnx_g_ref[...] + lnx_b_ref[...] + bonus_ref[0]) * gate_ref[0]
    o = o_ref[0]
    yb = o * lax.rsqrt(_head_sum(o * o, ones_bd) * inv_n + NORM_EPS) * sb_g_ref[...]
    cat = jnp.concatenate([ya, yb], axis=-1).astype(BF16)
    y2 = _dot(cat, w_out_ref[...])
    h = x_ref[0] + _rms(y2, post_ref[...])
    hn = _rms(h, pre_ref[...])
    h_o[0] = h
    hn_o[0] = hn.astype(BF16)
    if has_router:
        hi, lo = _split2(hn)
        r_hi, r_lo = _split2(router_ref[...])
        logits = _dot(hi, r_hi) + (_dot(lo, r_hi) + _dot(hi, r_lo))
        lane = lax.broadcasted_iota(jnp.int32, logits.shape, 1)
        neg = jnp.float32(-jnp.inf)
        logits = jnp.where(lane < N_EXPERTS, logits, neg)
        m1 = jnp.max(logits, axis=-1, keepdims=True)
        i1 = jnp.min(jnp.where(logits == m1, lane, LANES), axis=-1, keepdims=True)
        rest = jnp.where(lane == i1, neg, logits)
        m2 = jnp.max(rest, axis=-1, keepdims=True)
        i2 = jnp.min(jnp.where(rest == m2, lane, LANES), axis=-1, keepdims=True)
        e2 = jnp.exp(m2 - m1)
        g1 = 1.0 / (1.0 + e2)
        dg_o[0] = jnp.where(lane == i1, g1, jnp.where(lane == i2, e2 * g1, 0.0))
    else:
        dg_o[0] = jnp.zeros_like(dg_o[0])


def _mix_out(y, bonus, gate, o, x, p, tile):
    B, T, D = x.shape
    has_router = 'router' in p
    row = lambda a: a.reshape(1, -1).astype(F32)
    router = _pad_cols(p['router'], LANES) if has_router else jnp.zeros((D, LANES), F32)
    full = lambda shape: pl.BlockSpec(shape, lambda b, t: (0,) * len(shape))
    tok = lambda width: pl.BlockSpec((1, tile, width), lambda b, t: (b, t, 0))
    return pl.pallas_call(
        functools.partial(_mix_out_kernel, has_router),
        grid=(B, T // tile),
        in_specs=[tok(RWKV_W)] * 4 + [tok(D)] +
                 [full((1, RWKV_W))] * 3 + [full((D, D)), full((1, D)), full((1, D)), full((D, LANES))],
        out_specs=[tok(D), tok(D), tok(LANES)],
        out_shape=[jax.ShapeDtypeStruct((B, T, D), F32), jax.ShapeDtypeStruct((B, T, D), BF16),
                   jax.ShapeDtypeStruct((B, T, LANES), F32)],
        compiler_params=pltpu.CompilerParams(
            dimension_semantics=("arbitrary", "arbitrary"), vmem_limit_bytes=VMEM_LIMIT),
        name="mix_out",
    )(y, bonus, gate, o, x, row(p['lnx_g']), row(p['lnx_b']), row(p['sb_g']),
      p['w_out'].astype(BF16), row(p['mix_post']), row(p['ffn_pre']), router)


def _ffn_kernel(h_ref, hn_ref, wg_ref, wu_ref, wd_ref, post_ref, o_ref, acc_ref):
    f = pl.program_id(1)

    @pl.when(f == 0)
    def _():
        acc_ref[...] = jnp.zeros_like(acc_ref)

    hn = hn_ref[...]
    g = _dot(hn, wg_ref[...])
    u = _dot(hn, wu_ref[...])
    act = (g * jax.nn.sigmoid(g) * u).astype(BF16)
    acc_ref[...] += _dot(act, wd_ref[...])

    @pl.when(f == pl.num_programs(1) - 1)
    def _():
        o_ref[...] = h_ref[...] + _rms(acc_ref[...], post_ref[...])


def _ffn(h, hn, w_gate, w_up, w_down, post, tile_m, tile_f):
    N, D = h.shape
    F = w_gate.shape[1]
    return pl.pallas_call(
        _ffn_kernel,
        grid=(N // tile_m, F // tile_f),
        in_specs=[pl.BlockSpec((tile_m, D), lambda i, f: (i, 0)),
                  pl.BlockSpec((tile_m, D), lambda i, f: (i, 0)),
                  pl.BlockSpec((D, tile_f), lambda i, f: (0, f)),
                  pl.BlockSpec((D, tile_f), lambda i, f: (0, f)),
                  pl.BlockSpec((tile_f, D), lambda i, f: (f, 0)),
                  pl.BlockSpec((1, D), lambda i, f: (0, 0))],
        out_specs=pl.BlockSpec((tile_m, D), lambda i, f: (i, 0)),
        out_shape=jax.ShapeDtypeStruct((N, D), F32),
        scratch_shapes=[pltpu.VMEM((tile_m, D), F32)],
        compiler_params=pltpu.CompilerParams(
            dimension_semantics=("arbitrary", "arbitrary"), vmem_limit_bytes=VMEM_LIMIT),
        name="ffn",
    )(h, hn, w_gate.astype(BF16), w_up.astype(BF16), w_down.astype(BF16),
      post.reshape(1, -1).astype(F32))


def _moe_kernel(h_ref, hn_ref, dg_ref, wg_ref, wu_ref, wd_ref, post_ref, o_ref, acc_ref):
    e = pl.program_id(1)
    f = pl.program_id(2)

    @pl.when(jnp.logical_and(e == 0, f == 0))
    def _():
        acc_ref[...] = jnp.zeros_like(acc_ref)

    hn = hn_ref[...]
    dg = dg_ref[...]
    lane = lax.broadcasted_iota(jnp.int32, dg.shape, 1)
    gate_e = jnp.sum(jnp.where(lane == e, dg, 0.0), axis=-1, keepdims=True)
    g = _dot(hn, wg_ref[0])
    u = _dot(hn, wu_ref[0])
    act = (g * jax.nn.sigmoid(g) * u * gate_e).astype(BF16)
    acc_ref[...] += _dot(act, wd_ref[0])

    @pl.when(jnp.logical_and(e == pl.num_programs(1) - 1, f == pl.num_programs(2) - 1))
    def _():
        o_ref[...] = h_ref[...] + _rms(acc_ref[...], post_ref[...])


def _moe(h, hn, dg, w_gate, w_up, w_down, post, tile_m, tile_f):
    N, D = h.shape
    E, _, F = w_gate.shape
    return pl.pallas_call(
        _moe_kernel,
        grid=(N // tile_m, E, F // tile_f),
        in_specs=[pl.BlockSpec((tile_m, D), lambda i, e, f: (i, 0)),
                  pl.BlockSpec((tile_m, D), lambda i, e, f: (i, 0)),
                  pl.BlockSpec((tile_m, LANES), lambda i, e, f: (i, 0)),
                  pl.BlockSpec((1, D, tile_f), lambda i, e, f: (e, 0, f)),
                  pl.BlockSpec((1, D, tile_f), lambda i, e, f: (e, 0, f)),
                  pl.BlockSpec((1, tile_f, D), lambda i, e, f: (e, f, 0)),
                  pl.BlockSpec((1, D), lambda i, e, f: (0, 0))],
        out_specs=pl.BlockSpec((tile_m, D), lambda i, e, f: (i, 0)),
        out_shape=jax.ShapeDtypeStruct((N, D), F32),
        scratch_shapes=[pltpu.VMEM((tile_m, D), F32)],
        compiler_params=pltpu.CompilerParams(
            dimension_semantics=("arbitrary", "arbitrary", "arbitrary"),
            vmem_limit_bytes=VMEM_LIMIT),
        name="moe",
    )(h, hn, dg, w_gate.astype(BF16), w_up.astype(BF16), w_down.astype(BF16),
      post.reshape(1, -1).astype(F32))


MIX_TILE = 256
RWKV_TILE = 512
SB_BLOCK = 256
FFN_TILE_M = 512
FFN_TILE_F = 512


def _layer(x, p, v_first):
    B, T, D = x.shape
    r, lw, k, v, kk, a, gate, bonus, q, sk, sv = _mix_in(x, p, v_first, MIX_TILE)
    y = _rwkv_scan(r, lw, k, v, kk, a, RWKV_TILE)
    o = _sb_attention(q, sk, sv, SB_BLOCK)
    h, hn, dg = _mix_out(y, bonus, gate, o, x, p, MIX_TILE)
    h2, hn2 = h.reshape(B * T, D), hn.reshape(B * T, D)
    if 'router' in p:
        out = _moe(h2, hn2, dg.reshape(B * T, LANES), p['moe_gate'], p['moe_up'], p['moe_down'],
                   p['ffn_post'], FFN_TILE_M, FFN_TILE_F)
    else:
        out = _ffn(h2, hn2, p['ff_gate'], p['ff_up'], p['ff_down'], p['ffn_post'],
                   FFN_TILE_M, FFN_TILE_F)
    return out.reshape(B, T, D), (v if v_first is None else v_first)


def kernel(x, mix_pre_0, w_in_0, mu_0, decay0_0, decay_up_0, iclr0_0, iclr_up_0, gate_up_0, k_k_0, k_a_0, r_k_0, lnx_g_0, lnx_b_0, sb_g_0, w_out_0, mix_post_0, ffn_pre_0, ff_gate_0, ff_up_0, ff_down_0, ffn_post_0, mix_pre_1, w_in_1, mu_1, decay0_1, decay_up_1, iclr0_1, iclr_up_1, gate_up_1, vres0_1, vres_up_1, k_k_1, k_a_1, r_k_1, lnx_g_1, lnx_b_1, sb_g_1, w_out_1, mix_post_1, ffn_pre_1, router_1, moe_gate_1, moe_up_1, moe_down_1, ffn_post_1):
    p0 = dict(mix_pre=mix_pre_0, w_in=w_in_0, mu=mu_0, decay0=decay0_0, decay_up=decay_up_0,
              iclr0=iclr0_0, iclr_up=iclr_up_0, gate_up=gate_up_0, k_k=k_k_0, k_a=k_a_0,
              r_k=r_k_0, lnx_g=lnx_g_0, lnx_b=lnx_b_0, sb_g=sb_g_0, w_out=w_out_0,
              mix_post=mix_post_0, ffn_pre=ffn_pre_0, ff_gate=ff_gate_0, ff_up=ff_up_0,
              ff_down=ff_down_0, ffn_post=ffn_post_0)
    p1 = dict(mix_pre=mix_pre_1, w_in=w_in_1, mu=mu_1, decay0=decay0_1, decay_up=decay_up_1,
              iclr0=iclr0_1, iclr_up=iclr_up_1, gate_up=gate_up_1, vres0=vres0_1,
              vres_up=vres_up_1, k_k=k_k_1, k_a=k_a_1, r_k=r_k_1, lnx_g=lnx_g_1, lnx_b=lnx_b_1,
              sb_g=sb_g_1, w_out=w_out_1, mix_post=mix_post_1, ffn_pre=ffn_pre_1,
              router=router_1, moe_gate=moe_gate_1, moe_up=moe_up_1, moe_down=moe_down_1,
              ffn_post=ffn_post_1)
    x, v_first = _layer(x, p0, None)
    x, _ = _layer(x, p1, v_first)
    return x
```

```python
import functools
import math

import jax
import jax.numpy as jnp
from jax import lax
from jax.experimental import pallas as pl
from jax.experimental.pallas import tpu as pltpu

HEAD_DIM = 64
RWKV_W = 512
SB_W = 512
N_EXPERTS = 8
NORM_EPS = 1e-6
LNX_EPS = 64e-5
DECAY_SCALE = math.exp(-0.5)
LANES = 128
CHUNK = 64
LOW_SLOT = 128
ROW_SLABS = 8
VMEM_LIMIT = 56 * 1024 * 1024

F32 = jnp.float32
BF16 = jnp.bfloat16


def _dot(a, b):
    return jnp.dot(a, b, preferred_element_type=F32)


def _dot_nt(a, b):
    return lax.dot_general(a, b, (((1,), (1,)), ((), ())), preferred_element_type=F32)


def _dot_tn(a, b):
    return lax.dot_general(a, b, (((0,), (0,)), ((), ())), preferred_element_type=F32)


def _head_ones(width):
    r = lax.broadcasted_iota(jnp.int32, (width, width), 0) // HEAD_DIM
    c = lax.broadcasted_iota(jnp.int32, (width, width), 1) // HEAD_DIM
    return jnp.where(r == c, 1.0, 0.0).astype(BF16)


def _head_sum(x, ones_bd):
    hi = x.astype(BF16)
    lo = (x - hi.astype(F32)).astype(BF16)
    return _dot(hi, ones_bd) + _dot(lo, ones_bd)


def _mix_in_kernel(has_vres, x_ref, g_ref, w_ref, mu_ref, decay0_ref, decay_up_ref,
                   iclr0_ref, iclr_up_ref, gate_up_ref, vres0_ref, vres_up_ref,
                   kk_g_ref, ka_ref, rk_ref, vfirst_ref,
                   r_o, lw_o, k_o, v_o, kk_o, a_o, gate_o, bonus_o, q_o, sk_o, sv_o,
                   carry_ref):
    t = pl.program_id(1)
    n_shift = 3 * RWKV_W + 4 * LOW_SLOT

    @pl.when(t == 0)
    def _():
        carry_ref[...] = jnp.zeros_like(carry_ref)

    x = x_ref[0]
    hn = x * lax.rsqrt(jnp.mean(x * x, axis=-1, keepdims=True) + NORM_EPS) * g_ref[...]
    p = _dot(hn.astype(BF16), w_ref[...])
    pr = p[:, :n_shift]
    rows = pr.shape[0]
    rolled = pltpu.roll(pr, 1, 0)
    first = lax.broadcasted_iota(jnp.int32, pr.shape, 0) == 0
    prev = jnp.where(first, carry_ref[0:1, :], rolled)
    carry_ref[0:1, :] = pr[rows - 1:rows, :]
    s = pr + mu_ref[...] * (prev - pr)

    r = s[:, 0:RWKV_W]
    k = s[:, RWKV_W:2 * RWKV_W]
    v = s[:, 2 * RWKV_W:3 * RWKV_W]
    base = 3 * RWKV_W
    w_dn = s[:, base:base + LOW_SLOT]
    a_dn = s[:, base + LOW_SLOT:base + 2 * LOW_SLOT]
    g_dn = s[:, base + 2 * LOW_SLOT:base + 3 * LOW_SLOT]
    logw = -DECAY_SCALE * jax.nn.sigmoid(
        decay0_ref[...] + _dot(jnp.tanh(w_dn).astype(BF16), decay_up_ref[...]))
    iclr = jax.nn.sigmoid(iclr0_ref[...] + _dot(a_dn.astype(BF16), iclr_up_ref[...]))
    gate = _dot(jax.nn.sigmoid(g_dn).astype(BF16), gate_up_ref[...])
    if has_vres:
        vr_dn = s[:, base + 3 * LOW_SLOT:base + 4 * LOW_SLOT]
        mix = jax.nn.sigmoid(vres0_ref[...] + _dot(vr_dn.astype(BF16), vres_up_ref[...]))
        v = v + (vfirst_ref[0] - v) * mix

    ones_bd = _head_ones(RWKV_W)
    kk = k * kk_g_ref[...]
    kk = kk * lax.rsqrt(jnp.maximum(_head_sum(kk * kk, ones_bd), 1e-12))
    k = k * (1.0 + (iclr - 1.0) * ka_ref[...])
    bonus = _head_sum(r * k * rk_ref[...], ones_bd) * v

    r_o[0] = r
    lw_o[0] = logw
    k_o[0] = k
    v_o[0] = v
    kk_o[0] = kk
    a_o[0] = iclr
    gate_o[0] = gate
    bonus_o[0] = bonus
    q_o[0] = (p[:, n_shift:n_shift + SB_W] * (HEAD_DIM ** -0.5)).astype(BF16)
    sk_o[0] = p[:, n_shift + SB_W:n_shift + 2 * SB_W].astype(BF16)
    sv_o[0] = p[:, n_shift + 2 * SB_W:n_shift + 3 * SB_W].astype(BF16)


def _pad_cols(a, width):
    return jnp.pad(a, ((0, 0), (0, width - a.shape[1])))


def _pad_rows(a, height):
    return jnp.pad(a, ((0, height - a.shape[0]), (0, 0)))


def _mix_in(x, p, v_first, tile):
    B, T, D = x.shape
    has_vres = v_first is not None
    w_in, mu = p['w_in'], p['mu']
    ranks = [32, 32, 96] + ([32] if has_vres else [])
    off = 3 * RWKV_W
    low_w, low_mu = [], []
    for rk in ranks:
        low_w.append(_pad_cols(w_in[:, off:off + rk], LOW_SLOT))
        low_mu.append(jnp.pad(mu[off:off + rk], (0, LOW_SLOT - rk)))
        off += rk
    if not has_vres:
        low_w.append(jnp.zeros((D, LOW_SLOT), F32))
        low_mu.append(jnp.zeros((LOW_SLOT,), F32))
    w_all = jnp.concatenate([w_in[:, :3 * RWKV_W]] + low_w + [w_in[:, off:]], axis=1).astype(BF16)
    mu_all = jnp.concatenate([mu[:3 * RWKV_W]] + low_mu)[None, :]
    n_shift = 3 * RWKV_W + 4 * LOW_SLOT
    n_cols = n_shift + 3 * SB_W
    assert w_all.shape == (D, n_cols)

    row = lambda a: a.reshape(1, -1).astype(F32)
    up = lambda a: _pad_rows(a, LOW_SLOT).astype(BF16)
    if has_vres:
        vres0, vres_up, vf = row(p['vres0']), up(p['vres_up']), v_first
    else:
        vres0 = jnp.zeros((1, RWKV_W), F32)
        vres_up = jnp.zeros((LOW_SLOT, RWKV_W), BF16)
        vf = jnp.zeros((B, 8, RWKV_W), F32)

    grid = (B, T // tile)
    full = lambda shape: pl.BlockSpec(shape, lambda b, t: (0,) * len(shape))
    tok = lambda width: pl.BlockSpec((1, tile, width), lambda b, t: (b, t, 0))
    vf_spec = tok(RWKV_W) if has_vres else pl.BlockSpec((1, 8, RWKV_W), lambda b, t: (b, 0, 0))
    f32_out = jax.ShapeDtypeStruct((B, T, RWKV_W), F32)
    bf_out = jax.ShapeDtypeStruct((B, T, SB_W), BF16)
    outs = pl.pallas_call(
        functools.partial(_mix_in_kernel, has_vres),
        grid=grid,
        in_specs=[tok(D), full((1, D)), full((D, n_cols)), full((1, n_shift)),
                  full((1, RWKV_W)), full((LOW_SLOT, RWKV_W)),
                  full((1, RWKV_W)), full((LOW_SLOT, RWKV_W)),
                  full((LOW_SLOT, RWKV_W)),
                  full((1, RWKV_W)), full((LOW_SLOT, RWKV_W)),
                  full((1, RWKV_W)), full((1, RWKV_W)), full((1, RWKV_W)),
                  vf_spec],
        out_specs=[tok(RWKV_W)] * 8 + [tok(SB_W)] * 3,
        out_shape=[f32_out] * 8 + [bf_out] * 3,
        scratch_shapes=[pltpu.VMEM((8, n_shift), F32)],
        compiler_params=pltpu.CompilerParams(
            dimension_semantics=("arbitrary", "arbitrary"), vmem_limit_bytes=VMEM_LIMIT),
        name="mix_in",
    )(x, row(p['mix_pre']), w_all, mu_all,
      row(p['decay0']), up(p['decay_up']), row(p['iclr0']), up(p['iclr_up']), up(p['gate_up']),
      vres0, vres_up, row(p['k_k']), row(p['k_a']), row(p['r_k']), vf)
    return outs


def _stack(q, m0, m1):
    return jnp.concatenate([q * m0, q * m1], axis=0)


def _rwkv_kernel(n_chunks, r_ref, lw_ref, k_ref, v_ref, kk_ref, a_ref, y_ref,
                 s_ref, rhat_sc, yin_sc, amat_sc, dmat_sc, decay_sc):
    t = pl.program_id(2)
    cur = lax.rem(t, 2)
    prev = 1 - cur
    C = CHUNK

    @pl.when(t == 0)
    def _():
        s_ref[...] = jnp.zeros_like(s_ref)
        rhat_sc[1] = jnp.zeros_like(rhat_sc[1])
        yin_sc[1] = jnp.zeros_like(yin_sc[1])
        amat_sc[1] = jnp.zeros_like(amat_sc[1])
        dmat_sc[1] = jnp.zeros_like(dmat_sc[1])
        decay_sc[1] = jnp.zeros_like(decay_sc[1])

    scan = {'s': s_ref[...], 'done': 0}
    n_slots = 8

    def scan_steps(slot):
        upto = ((slot + 1) * n_chunks) // n_slots
        for c in range(scan['done'], upto):
            s = scan['s']
            s_b = s.astype(BF16)
            y_st = _dot_nt(rhat_sc[prev, c], s_b)
            y_ref[0, c * C:(c + 1) * C, :] = y_st[:C, :] + y_st[C:, :] + yin_sc[prev, c]
            scan['s'] = (s * decay_sc[prev, c][0:1, :] + _dot(s_b, amat_sc[prev, c])
                         + dmat_sc[prev, c])
        scan['done'] = upto

    C2 = 2 * C
    lane = lax.broadcasted_iota(jnp.int32, (1, LANES), 1)
    m0 = jnp.where(lane < HEAD_DIM, 1.0, 0.0)
    m1 = 1.0 - m0
    ti = lax.broadcasted_iota(jnp.int32, (C, C), 0)
    tj = lax.broadcasted_iota(jnp.int32, (C, C), 1)
    tri = jnp.where(tj <= ti, 1.0, 0.0).astype(BF16)
    r2 = lax.broadcasted_iota(jnp.int32, (C2, C2), 0)
    c2 = lax.broadcasted_iota(jnp.int32, (C2, C2), 1)
    same_head = (r2 // C) == (c2 // C)
    strict = jnp.logical_and(same_head, c2 < r2)
    incl = jnp.logical_and(same_head, c2 <= r2)
    eye = jnp.where(r2 == c2, 1.0, 0.0)

    def bf(a):
        return a.astype(BF16)

    r = r_ref[0]
    lw = lw_ref[0]
    k = k_ref[0]
    v = v_ref[0]
    kk = kk_ref[0]
    b = kk * a_ref[0]

    lw_hi = bf(lw)
    lw_mid = bf(lw - lw_hi.astype(F32))
    lw_lo = bf(lw - lw_hi.astype(F32) - lw_mid.astype(F32))
    cums = []
    for c in range(n_chunks):
        sl = slice(c * C, (c + 1) * C)
        cums.append(_dot(tri, lw_hi[sl]) + (_dot(tri, lw_mid[sl]) + _dot(tri, lw_lo[sl])))
    cum = jnp.concatenate(cums, axis=0)
    total = jnp.concatenate(
        [jnp.broadcast_to(cc[C - 1:C, :], (C, LANES)) for cc in cums], axis=0)
    scan_steps(0)
    e_neg = jnp.exp(-cum)
    e_tail = jnp.exp(total - cum)
    rt_all = r * jnp.exp(cum)
    zt_all = -kk * jnp.exp(cum - lw)
    bt_all = b * e_neg
    kt_all = k * e_neg
    bh_all = b * e_tail
    kh_all = k * e_tail
    decay_all = jnp.exp(total)

    chunks = range(n_chunks)
    cut = lambda x, c: _stack(x[c * C:(c + 1) * C, :], m0, m1)
    rt = [cut(rt_all, c) for c in chunks]
    zt = [cut(zt_all, c) for c in chunks]
    bh = [bf(cut(bh_all, c)) for c in chunks]
    kh = [bf(cut(kh_all, c)) for c in chunks]
    vs = [bf(cut(v, c)) for c in chunks]
    g = [_dot_nt(bf(jnp.concatenate([zt[c], rt[c]], axis=0)),
                 bf(jnp.concatenate([cut(bt_all, c), cut(kt_all, c)], axis=0))) for c in chunks]
    scan_steps(1)
    l_zb = [jnp.where(strict, g[c][:C2, :C2], 0.0) for c in chunks]
    l_zk = [bf(jnp.where(strict, g[c][:C2, C2:], 0.0)) for c in chunks]
    l_rb = [bf(jnp.where(incl, g[c][C2:, :C2], 0.0)) for c in chunks]
    l_rk = [bf(jnp.where(incl, g[c][C2:, C2:], 0.0)) for c in chunks]

    inv = [eye + l_zb[c] for c in chunks]
    pw = l_zb
    for i in range(int(math.log2(C)) - 1):
        pw_b = [bf(p) for p in pw]
        pw = [_dot(p, p) for p in pw_b]
        inv = [inv[c] + _dot(bf(inv[c]), bf(pw[c])) for c in chunks]
        scan_steps(2 + i)

    zk_v = [_dot(l_zk[c], vs[c]) for c in chunks]
    zp = [_dot(bf(inv[c]), bf(jnp.concatenate([zt[c], zk_v[c]], axis=1))) for c in chunks]
    scan_steps(n_slots - 1)
    s_ref[...] = scan['s']
    zp_b = [bf(x) for x in zp]
    rk_v = [_dot(l_rk[c], vs[c]) for c in chunks]
    ry = [_dot(l_rb[c], zp_b[c]) for c in chunks]
    r_hat = [bf(rt[c] + ry[c][:, :LANES]) for c in chunks]
    y_in = [rk_v[c] + ry[c][:, LANES:] for c in chunks]
    a_mat = [bf(_dot_tn(zp_b[c][:, :LANES], bh[c])) for c in chunks]
    d_mat = [_dot_tn(jnp.concatenate([zp_b[c][:, LANES:], vs[c]], axis=0),
                     jnp.concatenate([bh[c], kh[c]], axis=0)) for c in chunks]

    for c in chunks:
        rhat_sc[cur, c] = r_hat[c]
        yin_sc[cur, c] = y_in[c][:C, :] + y_in[c][C:, :]
        amat_sc[cur, c] = a_mat[c]
        dmat_sc[cur, c] = d_mat[c]
        decay_sc[cur, c] = decay_all[c * C:c * C + 8, :]


def _rwkv_scan(r, lw, k, v, kk, a, tile):
    B, T, W = r.shape
    n_pairs = W // LANES
    n_tiles = T // tile
    n_chunks = tile // CHUNK
    in_spec = pl.BlockSpec((1, tile, LANES), lambda b, h, t: (b, jnp.minimum(t, n_tiles - 1), h))
    out_spec = pl.BlockSpec((1, tile, LANES), lambda b, h, t: (b, jnp.maximum(t - 1, 0), h))
    return pl.pallas_call(
        functools.partial(_rwkv_kernel, n_chunks),
        grid=(B, n_pairs, n_tiles + 1),
        in_specs=[in_spec] * 6,
        out_specs=out_spec,
        out_shape=jax.ShapeDtypeStruct((B, T, W), F32),
        scratch_shapes=[pltpu.VMEM((LANES, LANES), F32),
                        pltpu.VMEM((2, n_chunks, 2 * CHUNK, LANES), BF16),
                        pltpu.VMEM((2, n_chunks, CHUNK, LANES), F32),
                        pltpu.VMEM((2, n_chunks, LANES, LANES), BF16),
                        pltpu.VMEM((2, n_chunks, LANES, LANES), F32),
                        pltpu.VMEM((2, n_chunks, 8, LANES), F32)],
        compiler_params=pltpu.CompilerParams(
            dimension_semantics=("arbitrary", "arbitrary", "arbitrary"),
            vmem_limit_bytes=VMEM_LIMIT),
        name="rwkv_scan",
    )(r, lw, k, v, kk, a)


SB_LOG_ZERO = -104.0


def _sb_kernel(blk, q_ref, k_ref, v_ref, o_ref, acc_ref, carry_ref):
    qi = pl.program_id(2)
    lane = lax.broadcasted_iota(jnp.int32, (1, LANES), 1)
    in_h0 = lane < HEAD_DIM
    q = q_ref[0]
    zero = jnp.zeros_like(q)
    q_heads = (jnp.where(in_h0, q, zero), jnp.where(in_h0, zero, q))
    row = lax.broadcasted_iota(jnp.int32, (blk, blk), 0)
    col = lax.broadcasted_iota(jnp.int32, (blk, blk), 1)
    causal = col < row
    tri = jnp.where(row > col, 1.0, 0.0).astype(BF16)

    heads = (0, 1)

    def one_block(j, masked):
        start = pl.multiple_of(j * blk, blk)
        kj = k_ref[0, pl.ds(start, blk), :]
        vj = v_ref[0, pl.ds(start, blk), :]
        z = [_dot_nt(q_heads[h], kj) for h in heads]
        sp = [jnp.maximum(x, 0.0) + jnp.log(1.0 + jnp.exp(-jnp.abs(x))) for x in z]
        lk = [-x for x in sp]
        if masked:
            lk = [jnp.where(causal, x, 0.0) for x in lk]
        hi = [x.astype(BF16) for x in lk]
        lo = [(lk[h] - hi[h].astype(F32)).astype(BF16) for h in heads]
        btw = [_dot(hi[h], tri) + _dot(lo[h], tri) for h in heads]
        carry = [carry_ref[h] for h in heads]
        att = [jnp.exp(z[h] - sp[h] + btw[h] + carry[h][:, 0:1]) for h in heads]
        if masked:
            att = [jnp.where(causal, x, 0.0) for x in att]
        for h in heads:
            acc_ref[h] += _dot(att[h].astype(BF16), vj)
        new_carry = [carry[h] + (btw[h][:, 0:1] + lk[h][:, 0:1]) for h in heads]
        for h in heads:
            carry_ref[h] = new_carry[h]
        return jnp.max(jnp.maximum(new_carry[0], new_carry[1]))

    acc_ref[...] = jnp.zeros_like(acc_ref)
    carry_ref[...] = jnp.zeros_like(carry_ref)
    top = one_block(qi, True)

    def cond(state):
        j, top = state
        return jnp.logical_and(j >= 0, top > SB_LOG_ZERO)

    def body(state):
        j, _ = state
        return j - 1, one_block(j, False)

    lax.while_loop(cond, body, (qi - 1, top))
    o_ref[0] = jnp.where(in_h0, acc_ref[0], acc_ref[1])


def _sb_attention(q, k, v, blk):
    B, T, W = q.shape
    n_pairs = W // LANES
    q_spec = pl.BlockSpec((1, blk, LANES), lambda b, h, t: (b, t, h))
    kv_spec = pl.BlockSpec((1, T, LANES), lambda b, h, t: (b, 0, h))
    return pl.pallas_call(
        functools.partial(_sb_kernel, blk),
        grid=(B, n_pairs, T // blk),
        in_specs=[q_spec, kv_spec, kv_spec],
        out_specs=q_spec,
        out_shape=jax.ShapeDtypeStruct((B, T, W), F32),
        scratch_shapes=[pltpu.VMEM((2, blk, LANES), F32), pltpu.VMEM((2, blk, LANES), F32)],
        compiler_params=pltpu.CompilerParams(
            dimension_semantics=("arbitrary", "arbitrary", "arbitrary"),
            vmem_limit_bytes=VMEM_LIMIT),
        name="sb_attention",
    )(q, k, v)


def _rms(x, g):
    return x * lax.rsqrt(jnp.mean(x * x, axis=-1, keepdims=True) + NORM_EPS) * g


def _split2(x):
    hi = x.astype(BF16)
    return hi, (x - hi.astype(F32)).astype(BF16)


def _mix_out_kernel(has_router, y_ref, bonus_ref, gate_ref, o_ref, x_ref,
                    lnx_g_ref, lnx_b_ref, sb_g_ref, w_out_ref, post_ref, pre_ref, router_ref,
                    h_o, hn_o, route_o):
    ones_bd = _head_ones(RWKV_W)
    inv_n = 1.0 / HEAD_DIM
    y = y_ref[0]
    mean = _head_sum(y, ones_bd) * inv_n
    yc = y - mean
    var = _head_sum(yc * yc, ones_bd) * inv_n
    ya = (yc * lax.rsqrt(var + LNX_EPS) * lnx_g_ref[...] + lnx_b_ref[...] + bonus_ref[0]) * gate_ref[0]
    o = o_ref[0]
    yb = o * lax.rsqrt(_head_sum(o * o, ones_bd) * inv_n + NORM_EPS) * sb_g_ref[...]
    cat = jnp.concatenate([ya, yb], axis=-1).astype(BF16)
    y2 = _dot(cat, w_out_ref[...])
    h = x_ref[0] + _rms(y2, post_ref[...])
    hn = _rms(h, pre_ref[...])
    h_o[0] = h
    if has_router:
        tile = hn.shape[0]
        for s in range(ROW_SLABS):
            hn_o[pl.ds(s, tile, stride=ROW_SLABS), :] = hn[:, s * LANES:(s + 1) * LANES]
        hi, lo = _split2(hn)
        r_hi, r_lo = _split2(router_ref[...])
        logits = _dot(hi, r_hi) + (_dot(lo, r_hi) + _dot(hi, r_lo))
        lane = lax.broadcasted_iota(jnp.int32, logits.shape, 1)
        neg = jnp.float32(-jnp.inf)
        logits = jnp.where(lane < N_EXPERTS, logits, neg)
        m1 = jnp.max(logits, axis=-1, keepdims=True)
        i1 = jnp.min(jnp.where(logits == m1, lane, LANES), axis=-1, keepdims=True)
        rest = jnp.where(lane == i1, neg, logits)
        m2 = jnp.max(rest, axis=-1, keepdims=True)
        i2 = jnp.min(jnp.where(rest == m2, lane, LANES), axis=-1, keepdims=True)
        e2 = jnp.exp(m2 - m1)
        g1 = 1.0 / (1.0 + e2)
        route = jnp.where(lane == 0, g1, jnp.where(lane == 1, e2 * g1, 0.0))
        route = jnp.where(lane == 2, i1.astype(F32), jnp.where(lane == 3, i2.astype(F32), route))
        route_o[0] = route
    else:
        hn_o[...] = hn.astype(BF16)
        route_o[0] = jnp.zeros_like(route_o[0])


def _mix_out(y, bonus, gate, o, x, p, tile):
    B, T, D = x.shape
    has_router = 'router' in p
    row = lambda a: a.reshape(1, -1).astype(F32)
    router = _pad_cols(p['router'], LANES) if has_router else jnp.zeros((D, LANES), F32)
    full = lambda shape: pl.BlockSpec(shape, lambda b, t: (0,) * len(shape))
    tok = lambda width: pl.BlockSpec((1, tile, width), lambda b, t: (b, t, 0))
    n_t = T // tile
    if has_router:
        assert D == ROW_SLABS * LANES
        hn_spec = pl.BlockSpec((tile * ROW_SLABS, LANES), lambda b, t: (b * n_t + t, 0))
        hn_shape = jax.ShapeDtypeStruct((B * T * ROW_SLABS, LANES), F32)
    else:
        hn_spec = pl.BlockSpec((tile, D), lambda b, t: (b * n_t + t, 0))
        hn_shape = jax.ShapeDtypeStruct((B * T, D), BF16)
    return pl.pallas_call(
        functools.partial(_mix_out_kernel, has_router),
        grid=(B, n_t),
        in_specs=[tok(RWKV_W)] * 4 + [tok(D)] +
                 [full((1, RWKV_W))] * 3 + [full((D, D)), full((1, D)), full((1, D)), full((D, LANES))],
        out_specs=[tok(D), hn_spec, tok(LANES)],
        out_shape=[jax.ShapeDtypeStruct((B, T, D), F32), hn_shape,
                   jax.ShapeDtypeStruct((B, T, LANES), F32)],
        compiler_params=pltpu.CompilerParams(
            dimension_semantics=("arbitrary", "arbitrary"), vmem_limit_bytes=VMEM_LIMIT),
        name="mix_out",
    )(y, bonus, gate, o, x, row(p['lnx_g']), row(p['lnx_b']), row(p['sb_g']),
      p['w_out'].astype(BF16), row(p['mix_post']), row(p['ffn_pre']), router)


def _ffn_kernel(h_ref, hn_ref, wg_ref, wu_ref, wd_ref, post_ref, o_ref, acc_ref):
    f = pl.program_id(1)

    @pl.when(f == 0)
    def _():
        acc_ref[...] = jnp.zeros_like(acc_ref)

    hn = hn_ref[...]
    g = _dot(hn, wg_ref[...])
    u = _dot(hn, wu_ref[...])
    act = (g * jax.nn.sigmoid(g) * u).astype(BF16)
    acc_ref[...] += _dot(act, wd_ref[...])

    @pl.when(f == pl.num_programs(1) - 1)
    def _():
        o_ref[...] = h_ref[...] + _rms(acc_ref[...], post_ref[...])


def _ffn(h, hn, w_gate, w_up, w_down, post, tile_m, tile_f):
    N, D = h.shape
    F = w_gate.shape[1]
    return pl.pallas_call(
        _ffn_kernel,
        grid=(N // tile_m, F // tile_f),
        in_specs=[pl.BlockSpec((tile_m, D), lambda i, f: (i, 0)),
                  pl.BlockSpec((tile_m, D), lambda i, f: (i, 0)),
                  pl.BlockSpec((D, tile_f), lambda i, f: (0, f)),
                  pl.BlockSpec((D, tile_f), lambda i, f: (0, f)),
                  pl.BlockSpec((tile_f, D), lambda i, f: (f, 0)),
                  pl.BlockSpec((1, D), lambda i, f: (0, 0))],
        out_specs=pl.BlockSpec((tile_m, D), lambda i, f: (i, 0)),
        out_shape=jax.ShapeDtypeStruct((N, D), F32),
        scratch_shapes=[pltpu.VMEM((tile_m, D), F32)],
        compiler_params=pltpu.CompilerParams(
            dimension_semantics=("arbitrary", "arbitrary"), vmem_limit_bytes=VMEM_LIMIT),
        name="ffn",
    )(h, hn, w_gate.astype(BF16), w_up.astype(BF16), w_down.astype(BF16),
      post.reshape(1, -1).astype(F32))


def _gather_rows_kernel(idx_ref, src_ref, out_ref, sem):
    n = out_ref.shape[0]

    def issue(i, carry):
        pltpu.make_async_copy(src_ref.at[idx_ref[0, 0, i]], out_ref.at[i], sem).start()
        return carry

    lax.fori_loop(0, n, issue, 0, unroll=8)
    pltpu.make_async_copy(src_ref.at[pl.ds(0, n)], out_ref, sem).wait()


def _gather_rows(src, idx, tile):
    n_src = src.shape[0] // ROW_SLABS
    n_out = idx.shape[0]
    n_tiles = n_out // tile
    out = pl.pallas_call(
        _gather_rows_kernel,
        grid=(n_tiles,),
        in_specs=[pl.BlockSpec((1, 1, tile), lambda i: (i, 0, 0), memory_space=pltpu.SMEM),
                  pl.BlockSpec(memory_space=pl.ANY)],
        out_specs=pl.BlockSpec((tile, ROW_SLABS, LANES), lambda i: (i, 0, 0)),
        out_shape=jax.ShapeDtypeStruct((n_out, ROW_SLABS, LANES), F32),
        scratch_shapes=[pltpu.SemaphoreType.DMA(())],
        compiler_params=pltpu.CompilerParams(
            dimension_semantics=("arbitrary",), vmem_limit_bytes=VMEM_LIMIT),
        name="gather_rows",
    )(idx.reshape(n_tiles, 1, tile), src.reshape(n_src, ROW_SLABS, LANES))
    return out.reshape(n_out * ROW_SLABS, LANES)


def _unslab(ref, rows):
    return jnp.concatenate(
        [ref[pl.ds(s, rows, stride=ROW_SLABS), :] for s in range(ROW_SLABS)], axis=-1)


def _moe_kernel(te_ref, nv_ref, x_ref, wg_ref, wu_ref, wd_ref, y_ref, acc_ref, xb_ref):
    i = pl.program_id(0)
    f = pl.program_id(1)
    rows = xb_ref.shape[0]
    valid = i < nv_ref[0]

    @pl.when(jnp.logical_and(valid, f == 0))
    def _():
        acc_ref[...] = jnp.zeros_like(acc_ref)
        xb_ref[...] = _unslab(x_ref, rows).astype(BF16)

    @pl.when(valid)
    def _():
        xb = xb_ref[...]
        g = _dot(xb, wg_ref[0])
        u = _dot(xb, wu_ref[0])
        act = (g * jax.nn.sigmoid(g) * u).astype(BF16)
        acc_ref[...] += _dot(act, wd_ref[0])

    @pl.when(f == pl.num_programs(1) - 1)
    def _():
        acc = jnp.where(valid, acc_ref[...], 0.0)
        for s in range(ROW_SLABS):
            y_ref[pl.ds(s, rows, stride=ROW_SLABS), :] = acc[:, s * LANES:(s + 1) * LANES]


def _moe_experts(x_sorted, tile_expert, n_valid, w_gate, w_up, w_down, tile_m, tile_f):
    E, D, F = w_gate.shape
    R = x_sorted.shape[0] // ROW_SLABS
    n_f = F // tile_f

    def w_col(i, f, te, nv):
        return (te[i], 0, jnp.where(i < nv[0], f, n_f - 1))

    def w_row(i, f, te, nv):
        return (te[i], jnp.where(i < nv[0], f, n_f - 1), 0)

    slab = pl.BlockSpec((tile_m * ROW_SLABS, LANES), lambda i, f, te, nv: (i, 0))
    return pl.pallas_call(
        _moe_kernel,
        grid_spec=pltpu.PrefetchScalarGridSpec(
            num_scalar_prefetch=2,
            grid=(R // tile_m, n_f),
            in_specs=[slab,
                      pl.BlockSpec((1, D, tile_f), w_col),
                      pl.BlockSpec((1, D, tile_f), w_col),
                      pl.BlockSpec((1, tile_f, D), w_row)],
            out_specs=slab,
            scratch_shapes=[pltpu.VMEM((tile_m, D), F32), pltpu.VMEM((tile_m, D), BF16)]),
        out_shape=jax.ShapeDtypeStruct((R * ROW_SLABS, LANES), F32),
        compiler_params=pltpu.CompilerParams(
            dimension_semantics=("arbitrary", "arbitrary"), vmem_limit_bytes=VMEM_LIMIT),
        name="moe_experts",
    )(tile_expert, n_valid, x_sorted, w_gate.astype(BF16), w_up.astype(BF16), w_down.astype(BF16))


def _moe_combine_kernel(h_ref, y1_ref, y2_ref, route_ref, post_ref, o_ref):
    rows = h_ref.shape[0]
    route = route_ref[...]
    f = route[:, 0:1] * _unslab(y1_ref, rows) + route[:, 1:2] * _unslab(y2_ref, rows)
    o_ref[...] = h_ref[...] + _rms(f, post_ref[...])


def _moe_combine(h, ys, route, post, tile):
    N, D = h.shape
    n_t = N // tile
    return pl.pallas_call(
        _moe_combine_kernel,
        grid=(n_t,),
        in_specs=[pl.BlockSpec((tile, D), lambda i: (i, 0)),
                  pl.BlockSpec((tile * ROW_SLABS, LANES), lambda i: (i, 0)),
                  pl.BlockSpec((tile * ROW_SLABS, LANES), lambda i: (n_t + i, 0)),
                  pl.BlockSpec((tile, LANES), lambda i: (i, 0)),
                  pl.BlockSpec((1, D), lambda i: (0, 0))],
        out_specs=pl.BlockSpec((tile, D), lambda i: (i, 0)),
        out_shape=jax.ShapeDtypeStruct((N, D), F32),
        compiler_params=pltpu.CompilerParams(
            dimension_semantics=("arbitrary",), vmem_limit_bytes=VMEM_LIMIT),
        name="moe_combine",
    )(h, ys, ys, route, post.reshape(1, -1).astype(F32))


def _moe(h, hn_slabs, route, w_gate, w_up, w_down, post, tile_m, tile_f):
    N, D = h.shape
    E = w_gate.shape[0]
    n_slots = 2 * N
    R = n_slots + E * tile_m
    n_tiles = R // tile_m

    experts = jnp.concatenate([route[:, 2], route[:, 3]]).astype(jnp.int32)
    onehot = (experts[:, None] == jnp.arange(E, dtype=jnp.int32)[None, :]).astype(jnp.int32)
    ranks = jnp.cumsum(onehot, axis=0) - onehot
    counts = jnp.sum(onehot, axis=0)
    padded = ((counts + tile_m - 1) // tile_m) * tile_m
    ends = jnp.cumsum(padded)
    pos = jnp.sum(onehot * (ranks + (ends - padded)[None, :]), axis=1)
    tokens = jnp.arange(n_slots, dtype=jnp.int32) % N
    token_of_row = jnp.zeros((R,), jnp.int32).at[pos].set(tokens)
    tile_start = jnp.arange(n_tiles, dtype=jnp.int32) * tile_m
    tile_expert = jnp.minimum(
        jnp.sum((ends[None, :] <= tile_start[:, None]).astype(jnp.int32), axis=1), E - 1)
    n_valid = (ends[E - 1] // tile_m).reshape(1).astype(jnp.int32)

    x_sorted = _gather_rows(hn_slabs, token_of_row, tile_m)
    y_sorted = _moe_experts(x_sorted, tile_expert, n_valid, w_gate, w_up, w_down, tile_m, tile_f)
    ys = _gather_rows(y_sorted, pos, tile_m)
    return _moe_combine(h, ys, route, post, tile_m)


MIX_TILE = 256
RWKV_TILE = 512
SB_BLOCK = 256
FFN_TILE_M = 512
FFN_TILE_F = 512


def _layer(x, p, v_first):
    B, T, D = x.shape
    r, lw, k, v, kk, a, gate, bonus, q, sk, sv = _mix_in(x, p, v_first, MIX_TILE)
    y = _rwkv_scan(r, lw, k, v, kk, a, RWKV_TILE)
    o = _sb_attention(q, sk, sv, SB_BLOCK)
    h, hn, route = _mix_out(y, bonus, gate, o, x, p, MIX_TILE)
    h2 = h.reshape(B * T, D)
    if 'router' in p:
        out = _moe(h2, hn, route.reshape(B * T, LANES), p['moe_gate'], p['moe_up'], p['moe_down'],
                   p['ffn_post'], FFN_TILE_M, FFN_TILE_F)
    else:
        out = _ffn(h2, hn, p['ff_gate'], p['ff_up'], p['ff_down'], p['ffn_post'],
                   FFN_TILE_M, FFN_TILE_F)
    return out.reshape(B, T, D), (v if v_first is None else v_first)


def kernel(x, mix_pre_0, w_in_0, mu_0, decay0_0, decay_up_0, iclr0_0, iclr_up_0, gate_up_0, k_k_0, k_a_0, r_k_0, lnx_g_0, lnx_b_0, sb_g_0, w_out_0, mix_post_0, ffn_pre_0, ff_gate_0, ff_up_0, ff_down_0, ffn_post_0, mix_pre_1, w_in_1, mu_1, decay0_1, decay_up_1, iclr0_1, iclr_up_1, gate_up_1, vres0_1, vres_up_1, k_k_1, k_a_1, r_k_1, lnx_g_1, lnx_b_1, sb_g_1, w_out_1, mix_post_1, ffn_pre_1, router_1, moe_gate_1, moe_up_1, moe_down_1, ffn_post_1):
    p0 = dict(mix_pre=mix_pre_0, w_in=w_in_0, mu=mu_0, decay0=decay0_0, decay_up=decay_up_0,
              iclr0=iclr0_0, iclr_up=iclr_up_0, gate_up=gate_up_0, k_k=k_k_0, k_a=k_a_0,
              r_k=r_k_0, lnx_g=lnx_g_0, lnx_b=lnx_b_0, sb_g=sb_g_0, w_out=w_out_0,
              mix_post=mix_post_0, ffn_pre=ffn_pre_0, ff_gate=ff_gate_0, ff_up=ff_up_0,
              ff_down=ff_down_0, ffn_post=ffn_post_0)
    p1 = dict(mix_pre=mix_pre_1, w_in=w_in_1, mu=mu_1, decay0=decay0_1, decay_up=decay_up_1,
              iclr0=iclr0_1, iclr_up=iclr_up_1, gate_up=gate_up_1, vres0=vres0_1,
              vres_up=vres_up_1, k_k=k_k_1, k_a=k_a_1, r_k=r_k_1, lnx_g=lnx_g_1, lnx_b=lnx_b_1,
              sb_g=sb_g_1, w_out=w_out_1, mix_post=mix_post_1, ffn_pre=ffn_pre_1,
              router=router_1, moe_gate=moe_gate_1, moe_up=moe_up_1, moe_down=moe_down_1,
              ffn_post=ffn_post_1)
    x, v_first = _layer(x, p0, None)
    x, _ = _layer(x, p1, v_first)
    return x
```

```python
import functools
import math

import jax
import jax.numpy as jnp
from jax import lax
from jax.experimental import pallas as pl
from jax.experimental.pallas import tpu as pltpu

HEAD_DIM = 64
RWKV_W = 512
SB_W = 512
N_EXPERTS = 8
NORM_EPS = 1e-6
LNX_EPS = 64e-5
DECAY_SCALE = math.exp(-0.5)
LANES = 128
CHUNK = 64
LOW_SLOT = 128
ROW_SLABS = 8
VMEM_LIMIT = 56 * 1024 * 1024

F32 = jnp.float32
BF16 = jnp.bfloat16


def _dot(a, b):
    return jnp.dot(a, b, preferred_element_type=F32)


def _dot_nt(a, b):
    return lax.dot_general(a, b, (((1,), (1,)), ((), ())), preferred_element_type=F32)


def _dot_tn(a, b):
    return lax.dot_general(a, b, (((0,), (0,)), ((), ())), preferred_element_type=F32)


def _head_ones(width):
    r = lax.broadcasted_iota(jnp.int32, (width, width), 0) // HEAD_DIM
    c = lax.broadcasted_iota(jnp.int32, (width, width), 1) // HEAD_DIM
    return jnp.where(r == c, 1.0, 0.0).astype(BF16)


def _head_sum(x, ones_bd):
    hi = x.astype(BF16)
    lo = (x - hi.astype(F32)).astype(BF16)
    return _dot(hi, ones_bd) + _dot(lo, ones_bd)


def _mix_in_kernel(has_vres, x_ref, g_ref, w_ref, mu_ref, decay0_ref, decay_up_ref,
                   iclr0_ref, iclr_up_ref, gate_up_ref, vres0_ref, vres_up_ref,
                   kk_g_ref, ka_ref, rk_ref, vfirst_ref,
                   r_o, lw_o, k_o, v_o, kk_o, a_o, gate_o, bonus_o, q_o, sk_o, sv_o,
                   carry_ref):
    t = pl.program_id(1)
    n_shift = 3 * RWKV_W + 4 * LOW_SLOT

    @pl.when(t == 0)
    def _():
        carry_ref[...] = jnp.zeros_like(carry_ref)

    x = x_ref[0]
    hn = x * lax.rsqrt(jnp.mean(x * x, axis=-1, keepdims=True) + NORM_EPS) * g_ref[...]
    p = _dot(hn.astype(BF16), w_ref[...])
    pr = p[:, :n_shift]
    rows = pr.shape[0]
    rolled = pltpu.roll(pr, 1, 0)
    first = lax.broadcasted_iota(jnp.int32, pr.shape, 0) == 0
    prev = jnp.where(first, carry_ref[0:1, :], rolled)
    carry_ref[0:1, :] = pr[rows - 1:rows, :]
    s = pr + mu_ref[...] * (prev - pr)

    r = s[:, 0:RWKV_W]
    k = s[:, RWKV_W:2 * RWKV_W]
    v = s[:, 2 * RWKV_W:3 * RWKV_W]
    base = 3 * RWKV_W
    w_dn = s[:, base:base + LOW_SLOT]
    a_dn = s[:, base + LOW_SLOT:base + 2 * LOW_SLOT]
    g_dn = s[:, base + 2 * LOW_SLOT:base + 3 * LOW_SLOT]
    logw = -DECAY_SCALE * jax.nn.sigmoid(
        decay0_ref[...] + _dot(jnp.tanh(w_dn).astype(BF16), decay_up_ref[...]))
    iclr = jax.nn.sigmoid(iclr0_ref[...] + _dot(a_dn.astype(BF16), iclr_up_ref[...]))
    gate = _dot(jax.nn.sigmoid(g_dn).astype(BF16), gate_up_ref[...])
    if has_vres:
        vr_dn = s[:, base + 3 * LOW_SLOT:base + 4 * LOW_SLOT]
        mix = jax.nn.sigmoid(vres0_ref[...] + _dot(vr_dn.astype(BF16), vres_up_ref[...]))
        v = v + (vfirst_ref[0] - v) * mix

    ones_bd = _head_ones(RWKV_W)
    kk = k * kk_g_ref[...]
    kk = kk * lax.rsqrt(jnp.maximum(_head_sum(kk * kk, ones_bd), 1e-12))
    k = k * (1.0 + (iclr - 1.0) * ka_ref[...])
    bonus = _head_sum(r * k * rk_ref[...], ones_bd) * v

    r_o[0] = r
    lw_o[0] = logw
    k_o[0] = k
    v_o[0] = v
    kk_o[0] = kk
    a_o[0] = iclr
    gate_o[0] = gate
    bonus_o[0] = bonus
    q_o[0] = (p[:, n_shift:n_shift + SB_W] * (HEAD_DIM ** -0.5)).astype(BF16)
    sk_o[0] = p[:, n_shift + SB_W:n_shift + 2 * SB_W].astype(BF16)
    sv_o[0] = p[:, n_shift + 2 * SB_W:n_shift + 3 * SB_W].astype(BF16)


def _pad_cols(a, width):
    return jnp.pad(a, ((0, 0), (0, width - a.shape[1])))


def _pad_rows(a, height):
    return jnp.pad(a, ((0, height - a.shape[0]), (0, 0)))


def _mix_in(x, p, v_first, tile):
    B, T, D = x.shape
    has_vres = v_first is not None
    w_in, mu = p['w_in'], p['mu']
    ranks = [32, 32, 96] + ([32] if has_vres else [])
    off = 3 * RWKV_W
    low_w, low_mu = [], []
    for rk in ranks:
        low_w.append(_pad_cols(w_in[:, off:off + rk], LOW_SLOT))
        low_mu.append(jnp.pad(mu[off:off + rk], (0, LOW_SLOT - rk)))
        off += rk
    if not has_vres:
        low_w.append(jnp.zeros((D, LOW_SLOT), F32))
        low_mu.append(jnp.zeros((LOW_SLOT,), F32))
    w_all = jnp.concatenate([w_in[:, :3 * RWKV_W]] + low_w + [w_in[:, off:]], axis=1).astype(BF16)
    mu_all = jnp.concatenate([mu[:3 * RWKV_W]] + low_mu)[None, :]
    n_shift = 3 * RWKV_W + 4 * LOW_SLOT
    n_cols = n_shift + 3 * SB_W
    assert w_all.shape == (D, n_cols)

    row = lambda a: a.reshape(1, -1).astype(F32)
    up = lambda a: _pad_rows(a, LOW_SLOT).astype(BF16)
    if has_vres:
        vres0, vres_up, vf = row(p['vres0']), up(p['vres_up']), v_first
    else:
        vres0 = jnp.zeros((1, RWKV_W), F32)
        vres_up = jnp.zeros((LOW_SLOT, RWKV_W), BF16)
        vf = jnp.zeros((B, 8, RWKV_W), F32)

    grid = (B, T // tile)
    full = lambda shape: pl.BlockSpec(shape, lambda b, t: (0,) * len(shape))
    tok = lambda width: pl.BlockSpec((1, tile, width), lambda b, t: (b, t, 0))
    vf_spec = tok(RWKV_W) if has_vres else pl.BlockSpec((1, 8, RWKV_W), lambda b, t: (b, 0, 0))
    f32_out = jax.ShapeDtypeStruct((B, T, RWKV_W), F32)
    bf_out = jax.ShapeDtypeStruct((B, T, SB_W), BF16)
    outs = pl.pallas_call(
        functools.partial(_mix_in_kernel, has_vres),
        grid=grid,
        in_specs=[tok(D), full((1, D)), full((D, n_cols)), full((1, n_shift)),
                  full((1, RWKV_W)), full((LOW_SLOT, RWKV_W)),
                  full((1, RWKV_W)), full((LOW_SLOT, RWKV_W)),
                  full((LOW_SLOT, RWKV_W)),
                  full((1, RWKV_W)), full((LOW_SLOT, RWKV_W)),
                  full((1, RWKV_W)), full((1, RWKV_W)), full((1, RWKV_W)),
                  vf_spec],
        out_specs=[tok(RWKV_W)] * 8 + [tok(SB_W)] * 3,
        out_shape=[f32_out] * 8 + [bf_out] * 3,
        scratch_shapes=[pltpu.VMEM((8, n_shift), F32)],
        compiler_params=pltpu.CompilerParams(
            dimension_semantics=("arbitrary", "arbitrary"), vmem_limit_bytes=VMEM_LIMIT),
        name="mix_in",
    )(x, row(p['mix_pre']), w_all, mu_all,
      row(p['decay0']), up(p['decay_up']), row(p['iclr0']), up(p['iclr_up']), up(p['gate_up']),
      vres0, vres_up, row(p['k_k']), row(p['k_a']), row(p['r_k']), vf)
    return outs


def _stack(q, m0, m1):
    return jnp.concatenate([q * m0, q * m1], axis=0)


def _rwkv_kernel(n_chunks, r_ref, lw_ref, k_ref, v_ref, kk_ref, a_ref, y_ref,
                 s_ref, rhat_sc, yin_sc, amat_sc, dmat_sc, decay_sc):
    t = pl.program_id(2)
    cur = lax.rem(t, 2)
    prev = 1 - cur
    C = CHUNK

    @pl.when(t == 0)
    def _():
        s_ref[...] = jnp.zeros_like(s_ref)
        rhat_sc[1] = jnp.zeros_like(rhat_sc[1])
        yin_sc[1] = jnp.zeros_like(yin_sc[1])
        amat_sc[1] = jnp.zeros_like(amat_sc[1])
        dmat_sc[1] = jnp.zeros_like(dmat_sc[1])
        decay_sc[1] = jnp.zeros_like(decay_sc[1])

    scan = {'s': s_ref[...], 'done': 0}
    n_slots = 8

    def scan_steps(slot):
        upto = ((slot + 1) * n_chunks) // n_slots
        for c in range(scan['done'], upto):
            s = scan['s']
            s_b = s.astype(BF16)
            y_st = _dot_nt(rhat_sc[prev, c], s_b)
            y_ref[0, c * C:(c + 1) * C, :] = y_st[:C, :] + y_st[C:, :] + yin_sc[prev, c]
            scan['s'] = (s * decay_sc[prev, c][0:1, :] + _dot(s_b, amat_sc[prev, c])
                         + dmat_sc[prev, c])
        scan['done'] = upto

    C2 = 2 * C
    lane = lax.broadcasted_iota(jnp.int32, (1, LANES), 1)
    m0 = jnp.where(lane < HEAD_DIM, 1.0, 0.0)
    m1 = 1.0 - m0
    ti = lax.broadcasted_iota(jnp.int32, (C, C), 0)
    tj = lax.broadcasted_iota(jnp.int32, (C, C), 1)
    tri = jnp.where(tj <= ti, 1.0, 0.0).astype(BF16)
    r2 = lax.broadcasted_iota(jnp.int32, (C2, C2), 0)
    c2 = lax.broadcasted_iota(jnp.int32, (C2, C2), 1)
    same_head = (r2 // C) == (c2 // C)
    strict = jnp.logical_and(same_head, c2 < r2)
    incl = jnp.logical_and(same_head, c2 <= r2)
    eye = jnp.where(r2 == c2, 1.0, 0.0)

    def bf(a):
        return a.astype(BF16)

    r = r_ref[0]
    lw = lw_ref[0]
    k = k_ref[0]
    v = v_ref[0]
    kk = kk_ref[0]
    b = kk * a_ref[0]

    lw_hi = bf(lw)
    lw_mid = bf(lw - lw_hi.astype(F32))
    lw_lo = bf(lw - lw_hi.astype(F32) - lw_mid.astype(F32))
    cums = []
    for c in range(n_chunks):
        sl = slice(c * C, (c + 1) * C)
        cums.append(_dot(tri, lw_hi[sl]) + (_dot(tri, lw_mid[sl]) + _dot(tri, lw_lo[sl])))
    cum = jnp.concatenate(cums, axis=0)
    total = jnp.concatenate(
        [jnp.broadcast_to(cc[C - 1:C, :], (C, LANES)) for cc in cums], axis=0)
    scan_steps(0)
    e_neg = jnp.exp(-cum)
    e_tail = jnp.exp(total - cum)
    rt_all = r * jnp.exp(cum)
    zt_all = -kk * jnp.exp(cum - lw)
    bt_all = b * e_neg
    kt_all = k * e_neg
    bh_all = b * e_tail
    kh_all = k * e_tail
    decay_all = jnp.exp(total)

    chunks = range(n_chunks)
    cut = lambda x, c: _stack(x[c * C:(c + 1) * C, :], m0, m1)
    rt = [cut(rt_all, c) for c in chunks]
    zt = [cut(zt_all, c) for c in chunks]
    bh = [bf(cut(bh_all, c)) for c in chunks]
    kh = [bf(cut(kh_all, c)) for c in chunks]
    vs = [bf(cut(v, c)) for c in chunks]
    g = [_dot_nt(bf(jnp.concatenate([zt[c], rt[c]], axis=0)),
                 bf(jnp.concatenate([cut(bt_all, c), cut(kt_all, c)], axis=0))) for c in chunks]
    scan_steps(1)
    l_zb = [jnp.where(strict, g[c][:C2, :C2], 0.0) for c in chunks]
    l_zk = [bf(jnp.where(strict, g[c][:C2, C2:], 0.0)) for c in chunks]
    l_rb = [bf(jnp.where(incl, g[c][C2:, :C2], 0.0)) for c in chunks]
    l_rk = [bf(jnp.where(incl, g[c][C2:, C2:], 0.0)) for c in chunks]

    inv = [eye + l_zb[c] for c in chunks]
    pw = l_zb
    for i in range(int(math.log2(C)) - 1):
        pw_b = [bf(p) for p in pw]
        pw = [_dot(p, p) for p in pw_b]
        inv = [inv[c] + _dot(bf(inv[c]), bf(pw[c])) for c in chunks]
        scan_steps(2 + i)

    zk_v = [_dot(l_zk[c], vs[c]) for c in chunks]
    zp = [_dot(bf(inv[c]), bf(jnp.concatenate([zt[c], zk_v[c]], axis=1))) for c in chunks]
    scan_steps(n_slots - 1)
    s_ref[...] = scan['s']
    zp_b = [bf(x) for x in zp]
    rk_v = [_dot(l_rk[c], vs[c]) for c in chunks]
    ry = [_dot(l_rb[c], zp_b[c]) for c in chunks]
    r_hat = [bf(rt[c] + ry[c][:, :LANES]) for c in chunks]
    y_in = [rk_v[c] + ry[c][:, LANES:] for c in chunks]
    a_mat = [bf(_dot_tn(zp_b[c][:, :LANES], bh[c])) for c in chunks]
    d_mat = [_dot_tn(jnp.concatenate([zp_b[c][:, LANES:], vs[c]], axis=0),
                     jnp.concatenate([bh[c], kh[c]], axis=0)) for c in chunks]

    for c in chunks:
        rhat_sc[cur, c] = r_hat[c]
        yin_sc[cur, c] = y_in[c][:C, :] + y_in[c][C:, :]
        amat_sc[cur, c] = a_mat[c]
        dmat_sc[cur, c] = d_mat[c]
        decay_sc[cur, c] = decay_all[c * C:c * C + 8, :]


def _rwkv_scan(r, lw, k, v, kk, a, tile):
    B, T, W = r.shape
    n_pairs = W // LANES
    n_tiles = T // tile
    n_chunks = tile // CHUNK
    in_spec = pl.BlockSpec((1, tile, LANES), lambda b, h, t: (b, jnp.minimum(t, n_tiles - 1), h))
    out_spec = pl.BlockSpec((1, tile, LANES), lambda b, h, t: (b, jnp.maximum(t - 1, 0), h))
    return pl.pallas_call(
        functools.partial(_rwkv_kernel, n_chunks),
        grid=(B, n_pairs, n_tiles + 1),
        in_specs=[in_spec] * 6,
        out_specs=out_spec,
        out_shape=jax.ShapeDtypeStruct((B, T, W), F32),
        scratch_shapes=[pltpu.VMEM((LANES, LANES), F32),
                        pltpu.VMEM((2, n_chunks, 2 * CHUNK, LANES), BF16),
                        pltpu.VMEM((2, n_chunks, CHUNK, LANES), F32),
                        pltpu.VMEM((2, n_chunks, LANES, LANES), BF16),
                        pltpu.VMEM((2, n_chunks, LANES, LANES), F32),
                        pltpu.VMEM((2, n_chunks, 8, LANES), F32)],
        compiler_params=pltpu.CompilerParams(
            dimension_semantics=("arbitrary", "arbitrary", "arbitrary"),
            vmem_limit_bytes=VMEM_LIMIT),
        name="rwkv_scan",
    )(r, lw, k, v, kk, a)


SB_LOG_ZERO = -104.0
SB_ROW_PARTS = 2


def _sb_kernel(blk, q_ref, k_ref, v_ref, o_ref, acc_ref, carry_ref):
    qi = pl.program_id(2)
    lane = lax.broadcasted_iota(jnp.int32, (1, LANES), 1)
    in_h0 = lane < HEAD_DIM
    q = q_ref[0]
    zero = jnp.zeros_like(q)
    q_heads = (jnp.where(in_h0, q, zero), jnp.where(in_h0, zero, q))
    row = lax.broadcasted_iota(jnp.int32, (blk, blk), 0)
    col = lax.broadcasted_iota(jnp.int32, (blk, blk), 1)
    causal = col < row
    tri = jnp.where(row > col, 1.0, 0.0).astype(BF16)

    tri_sum = jnp.concatenate([tri, jnp.ones((blk, LANES), BF16)], axis=1)
    n_parts = SB_ROW_PARTS
    part = blk // n_parts
    chains = [(h, p) for h in (0, 1) for p in range(n_parts)]
    rows_of = lambda p: slice(p * part, (p + 1) * part)
    reps = blk // LANES

    def one_block(j, masked):
        start = pl.multiple_of(j * blk, blk)
        kj = k_ref[0, pl.ds(start, blk), :]
        vj = v_ref[0, pl.ds(start, blk), :]
        z = [_dot_nt(q_heads[h][rows_of(p)], kj) for h, p in chains]
        sp = [jnp.maximum(x, 0.0) + jnp.log(1.0 + jnp.exp(-jnp.abs(x))) for x in z]
        lk = [-x for x in sp]
        if masked:
            lk = [jnp.where(causal[rows_of(p)], x, 0.0) for (h, p), x in zip(chains, lk)]
        sums = [_dot(x.astype(BF16), tri_sum) for x in lk]
        carry = [carry_ref[h, rows_of(p), :] for h, p in chains]
        att = [jnp.exp(z[c] - sp[c] + sums[c][:, :blk] + jnp.concatenate([carry[c]] * reps, axis=1))
               for c in range(len(chains))]
        if masked:
            att = [jnp.where(causal[rows_of(p)], x, 0.0) for (h, p), x in zip(chains, att)]
        new_carry = [carry[c] + sums[c][:, blk:] for c in range(len(chains))]
        for c, (h, p) in enumerate(chains):
            acc_ref[h, rows_of(p), :] += _dot(att[c].astype(BF16), vj)
            carry_ref[h, rows_of(p), :] = new_carry[c]
        top = new_carry[0]
        for x in new_carry[1:]:
            top = jnp.maximum(top, x)
        return jnp.max(top)

    acc_ref[...] = jnp.zeros_like(acc_ref)
    carry_ref[...] = jnp.zeros_like(carry_ref)
    top = one_block(qi, True)

    def cond(state):
        j, top = state
        return jnp.logical_and(j >= 0, top > SB_LOG_ZERO)

    def body(state):
        j, _ = state
        return j - 1, one_block(j, False)

    lax.while_loop(cond, body, (qi - 1, top))
    o_ref[0] = jnp.where(in_h0, acc_ref[0], acc_ref[1])


def _sb_attention(q, k, v, blk):
    B, T, W = q.shape
    n_pairs = W // LANES
    q_spec = pl.BlockSpec((1, blk, LANES), lambda b, h, t: (b, t, h))
    kv_spec = pl.BlockSpec((1, T, LANES), lambda b, h, t: (b, 0, h))
    return pl.pallas_call(
        functools.partial(_sb_kernel, blk),
        grid=(B, n_pairs, T // blk),
        in_specs=[q_spec, kv_spec, kv_spec],
        out_specs=q_spec,
        out_shape=jax.ShapeDtypeStruct((B, T, W), F32),
        scratch_shapes=[pltpu.VMEM((2, blk, LANES), F32), pltpu.VMEM((2, blk, LANES), F32)],
        compiler_params=pltpu.CompilerParams(
            dimension_semantics=("arbitrary", "arbitrary", "arbitrary"),
            vmem_limit_bytes=VMEM_LIMIT),
        name="sb_attention",
    )(q, k, v)


def _rms(x, g):
    return x * lax.rsqrt(jnp.mean(x * x, axis=-1, keepdims=True) + NORM_EPS) * g


def _split2(x):
    hi = x.astype(BF16)
    return hi, (x - hi.astype(F32)).astype(BF16)


def _mix_out_kernel(has_router, y_ref, bonus_ref, gate_ref, o_ref, x_ref,
                    lnx_g_ref, lnx_b_ref, sb_g_ref, w_out_ref, post_ref, pre_ref, router_ref,
                    h_o, hn_o, route_o):
    ones_bd = _head_ones(RWKV_W)
    inv_n = 1.0 / HEAD_DIM
    y = y_ref[0]
    mean = _head_sum(y, ones_bd) * inv_n
    yc = y - mean
    var = _head_sum(yc * yc, ones_bd) * inv_n
    ya = (yc * lax.rsqrt(var + LNX_EPS) * lnx_g_ref[...] + lnx_b_ref[...] + bonus_ref[0]) * gate_ref[0]
    o = o_ref[0]
    yb = o * lax.rsqrt(_head_sum(o * o, ones_bd) * inv_n + NORM_EPS) * sb_g_ref[...]
    cat = jnp.concatenate([ya, yb], axis=-1).astype(BF16)
    y2 = _dot(cat, w_out_ref[...])
    h = x_ref[0] + _rms(y2, post_ref[...])
    hn = _rms(h, pre_ref[...])
    h_o[0] = h
    if has_router:
        tile = hn.shape[0]
        for s in range(ROW_SLABS):
            hn_o[pl.ds(s, tile, stride=ROW_SLABS), :] = hn[:, s * LANES:(s + 1) * LANES]
        hi, lo = _split2(hn)
        r_hi, r_lo = _split2(router_ref[...])
        logits = _dot(hi, r_hi) + (_dot(lo, r_hi) + _dot(hi, r_lo))
        lane = lax.broadcasted_iota(jnp.int32, logits.shape, 1)
        neg = jnp.float32(-jnp.inf)
        logits = jnp.where(lane < N_EXPERTS, logits, neg)
        m1 = jnp.max(logits, axis=-1, keepdims=True)
        i1 = jnp.min(jnp.where(logits == m1, lane, LANES), axis=-1, keepdims=True)
        rest = jnp.where(lane == i1, neg, logits)
        m2 = jnp.max(rest, axis=-1, keepdims=True)
        i2 = jnp.min(jnp.where(rest == m2, lane, LANES), axis=-1, keepdims=True)
        e2 = jnp.exp(m2 - m1)
        g1 = 1.0 / (1.0 + e2)
        route = jnp.where(lane == 0, g1, jnp.where(lane == 1, e2 * g1, 0.0))
        route = jnp.where(lane == 2, i1.astype(F32), jnp.where(lane == 3, i2.astype(F32), route))
        route_o[0] = route
    else:
        hn_o[...] = hn.astype(BF16)
        route_o[0] = jnp.zeros_like(route_o[0])


def _mix_out(y, bonus, gate, o, x, p, tile):
    B, T, D = x.shape
    has_router = 'router' in p
    row = lambda a: a.reshape(1, -1).astype(F32)
    router = _pad_cols(p['router'], LANES) if has_router else jnp.zeros((D, LANES), F32)
    full = lambda shape: pl.BlockSpec(shape, lambda b, t: (0,) * len(shape))
    tok = lambda width: pl.BlockSpec((1, tile, width), lambda b, t: (b, t, 0))
    n_t = T // tile
    if has_router:
        assert D == ROW_SLABS * LANES
        hn_spec = pl.BlockSpec((tile * ROW_SLABS, LANES), lambda b, t: (b * n_t + t, 0))
        hn_shape = jax.ShapeDtypeStruct((B * T * ROW_SLABS, LANES), F32)
    else:
        hn_spec = pl.BlockSpec((tile, D), lambda b, t: (b * n_t + t, 0))
        hn_shape = jax.ShapeDtypeStruct((B * T, D), BF16)
    return pl.pallas_call(
        functools.partial(_mix_out_kernel, has_router),
        grid=(B, n_t),
        in_specs=[tok(RWKV_W)] * 4 + [tok(D)] +
                 [full((1, RWKV_W))] * 3 + [full((D, D)), full((1, D)), full((1, D)), full((D, LANES))],
        out_specs=[tok(D), hn_spec, tok(LANES)],
        out_shape=[jax.ShapeDtypeStruct((B, T, D), F32), hn_shape,
                   jax.ShapeDtypeStruct((B, T, LANES), F32)],
        compiler_params=pltpu.CompilerParams(
            dimension_semantics=("arbitrary", "arbitrary"), vmem_limit_bytes=VMEM_LIMIT),
        name="mix_out",
    )(y, bonus, gate, o, x, row(p['lnx_g']), row(p['lnx_b']), row(p['sb_g']),
      p['w_out'].astype(BF16), row(p['mix_post']), row(p['ffn_pre']), router)


def _ffn_kernel(h_ref, hn_ref, wg_ref, wu_ref, wd_ref, post_ref, o_ref, acc_ref):
    f = pl.program_id(1)

    @pl.when(f == 0)
    def _():
        acc_ref[...] = jnp.zeros_like(acc_ref)

    hn = hn_ref[...]
    g = _dot(hn, wg_ref[...])
    u = _dot(hn, wu_ref[...])
    act = (g * jax.nn.sigmoid(g) * u).astype(BF16)
    acc_ref[...] += _dot(act, wd_ref[...])

    @pl.when(f == pl.num_programs(1) - 1)
    def _():
        o_ref[...] = h_ref[...] + _rms(acc_ref[...], post_ref[...])


def _ffn(h, hn, w_gate, w_up, w_down, post, tile_m, tile_f):
    N, D = h.shape
    F = w_gate.shape[1]
    return pl.pallas_call(
        _ffn_kernel,
        grid=(N // tile_m, F // tile_f),
        in_specs=[pl.BlockSpec((tile_m, D), lambda i, f: (i, 0)),
                  pl.BlockSpec((tile_m, D), lambda i, f: (i, 0)),
                  pl.BlockSpec((D, tile_f), lambda i, f: (0, f)),
                  pl.BlockSpec((D, tile_f), lambda i, f: (0, f)),
                  pl.BlockSpec((tile_f, D), lambda i, f: (f, 0)),
                  pl.BlockSpec((1, D), lambda i, f: (0, 0))],
        out_specs=pl.BlockSpec((tile_m, D), lambda i, f: (i, 0)),
        out_shape=jax.ShapeDtypeStruct((N, D), F32),
        scratch_shapes=[pltpu.VMEM((tile_m, D), F32)],
        compiler_params=pltpu.CompilerParams(
            dimension_semantics=("arbitrary", "arbitrary"), vmem_limit_bytes=VMEM_LIMIT),
        name="ffn",
    )(h, hn, w_gate.astype(BF16), w_up.astype(BF16), w_down.astype(BF16),
      post.reshape(1, -1).astype(F32))


GATHER_UNROLL = 8


def _gather_rows_kernel(idx_ref, src_ref, out_ref, sem):
    n = out_ref.shape[0]

    def issue(g, carry):
        for u in range(GATHER_UNROLL):
            i = g * GATHER_UNROLL + u
            pltpu.make_async_copy(src_ref.at[idx_ref[0, 0, i]], out_ref.at[i], sem).start(
                priority=u % 2)
        return carry

    lax.fori_loop(0, n // GATHER_UNROLL, issue, 0)
    pltpu.make_async_copy(src_ref.at[pl.ds(0, n)], out_ref, sem).wait()


def _gather_rows(src, idx, tile):
    n_src = src.shape[0] // ROW_SLABS
    n_out = idx.shape[0]
    n_tiles = n_out // tile
    out = pl.pallas_call(
        _gather_rows_kernel,
        grid=(n_tiles,),
        in_specs=[pl.BlockSpec((1, 1, tile), lambda i: (i, 0, 0), memory_space=pltpu.SMEM),
                  pl.BlockSpec(memory_space=pl.ANY)],
        out_specs=pl.BlockSpec((tile, ROW_SLABS, LANES), lambda i: (i, 0, 0)),
        out_shape=jax.ShapeDtypeStruct((n_out, ROW_SLABS, LANES), F32),
        scratch_shapes=[pltpu.SemaphoreType.DMA(())],
        compiler_params=pltpu.CompilerParams(
            dimension_semantics=("arbitrary",), vmem_limit_bytes=VMEM_LIMIT),
        name="gather_rows",
    )(idx.reshape(n_tiles, 1, tile), src.reshape(n_src, ROW_SLABS, LANES))
    return out.reshape(n_out * ROW_SLABS, LANES)


def _scatter_rows_kernel(idx_ref, src_ref, out_ref, sem):
    n = src_ref.shape[0]

    def issue(g, carry):
        for u in range(GATHER_UNROLL):
            i = g * GATHER_UNROLL + u
            pltpu.make_async_copy(src_ref.at[i], out_ref.at[idx_ref[0, 0, i]], sem).start(
                priority=u % 2)
        return carry

    lax.fori_loop(0, n // GATHER_UNROLL, issue, 0)
    pltpu.make_async_copy(src_ref, out_ref.at[pl.ds(0, n)], sem).wait()


def _scatter_rows(src, idx, tile):
    n_src = src.shape[0] // ROW_SLABS
    n_out = idx.shape[0]
    n_tiles = n_out // tile
    src_tiles = n_src // tile
    out = pl.pallas_call(
        _scatter_rows_kernel,
        grid=(n_tiles,),
        in_specs=[pl.BlockSpec((1, 1, tile), lambda i: (i, 0, 0), memory_space=pltpu.SMEM),
                  pl.BlockSpec((tile, ROW_SLABS, LANES), lambda i: (lax.rem(i, src_tiles), 0, 0))],
        out_specs=pl.BlockSpec(memory_space=pl.ANY),
        out_shape=jax.ShapeDtypeStruct((n_out, ROW_SLABS, LANES), F32),
        scratch_shapes=[pltpu.SemaphoreType.DMA(())],
        compiler_params=pltpu.CompilerParams(
            dimension_semantics=("arbitrary",), vmem_limit_bytes=VMEM_LIMIT),
        name="scatter_rows",
    )(idx.reshape(n_tiles, 1, tile), src.reshape(n_src, ROW_SLABS, LANES))
    return out.reshape(n_out * ROW_SLABS, LANES)


def _unslab(ref, rows):
    return jnp.concatenate(
        [ref[pl.ds(s, rows, stride=ROW_SLABS), :] for s in range(ROW_SLABS)], axis=-1)


def _moe_kernel(te_ref, nv_ref, x_ref, wg_ref, wu_ref, wd_ref, y_ref, acc_ref, xb_ref):
    i = pl.program_id(0)
    f = pl.program_id(1)
    rows = xb_ref.shape[0]
    valid = i < nv_ref[0]

    @pl.when(jnp.logical_and(valid, f == 0))
    def _():
        acc_ref[...] = jnp.zeros_like(acc_ref)
        xb_ref[...] = _unslab(x_ref, rows).astype(BF16)

    @pl.when(valid)
    def _():
        xb = xb_ref[...]
        g = _dot(xb, wg_ref[0])
        u = _dot(xb, wu_ref[0])
        act = (g * jax.nn.sigmoid(g) * u).astype(BF16)
        acc_ref[...] += _dot(act, wd_ref[0])

    @pl.when(f == pl.num_programs(1) - 1)
    def _():
        acc = jnp.where(valid, acc_ref[...], 0.0)
        for s in range(ROW_SLABS):
            y_ref[pl.ds(s, rows, stride=ROW_SLABS), :] = acc[:, s * LANES:(s + 1) * LANES]


def _moe_experts(x_sorted, tile_expert, n_valid, w_gate, w_up, w_down, tile_m, tile_f):
    E, D, F = w_gate.shape
    R = x_sorted.shape[0] // ROW_SLABS
    n_f = F // tile_f

    def w_col(i, f, te, nv):
        return (te[i], 0, jnp.where(i < nv[0], f, n_f - 1))

    def w_row(i, f, te, nv):
        return (te[i], jnp.where(i < nv[0], f, n_f - 1), 0)

    slab = pl.BlockSpec((tile_m * ROW_SLABS, LANES), lambda i, f, te, nv: (i, 0))
    return pl.pallas_call(
        _moe_kernel,
        grid_spec=pltpu.PrefetchScalarGridSpec(
            num_scalar_prefetch=2,
            grid=(R // tile_m, n_f),
            in_specs=[slab,
                      pl.BlockSpec((1, D, tile_f), w_col),
                      pl.BlockSpec((1, D, tile_f), w_col),
                      pl.BlockSpec((1, tile_f, D), w_row)],
            out_specs=slab,
            scratch_shapes=[pltpu.VMEM((tile_m, D), F32), pltpu.VMEM((tile_m, D), BF16)]),
        out_shape=jax.ShapeDtypeStruct((R * ROW_SLABS, LANES), F32),
        compiler_params=pltpu.CompilerParams(
            dimension_semantics=("arbitrary", "arbitrary"), vmem_limit_bytes=VMEM_LIMIT),
        name="moe_experts",
    )(tile_expert, n_valid, x_sorted, w_gate.astype(BF16), w_up.astype(BF16), w_down.astype(BF16))


def _moe_combine_kernel(h_ref, y1_ref, y2_ref, route_ref, post_ref, o_ref):
    rows = h_ref.shape[0]
    route = route_ref[...]
    f = route[:, 0:1] * _unslab(y1_ref, rows) + route[:, 1:2] * _unslab(y2_ref, rows)
    o_ref[...] = h_ref[...] + _rms(f, post_ref[...])


def _moe_combine(h, ys, route, post, tile):
    N, D = h.shape
    n_t = N // tile
    return pl.pallas_call(
        _moe_combine_kernel,
        grid=(n_t,),
        in_specs=[pl.BlockSpec((tile, D), lambda i: (i, 0)),
                  pl.BlockSpec((tile * ROW_SLABS, LANES), lambda i: (i, 0)),
                  pl.BlockSpec((tile * ROW_SLABS, LANES), lambda i: (n_t + i, 0)),
                  pl.BlockSpec((tile, LANES), lambda i: (i, 0)),
                  pl.BlockSpec((1, D), lambda i: (0, 0))],
        out_specs=pl.BlockSpec((tile, D), lambda i: (i, 0)),
        out_shape=jax.ShapeDtypeStruct((N, D), F32),
        compiler_params=pltpu.CompilerParams(
            dimension_semantics=("arbitrary",), vmem_limit_bytes=VMEM_LIMIT),
        name="moe_combine",
    )(h, ys, ys, route, post.reshape(1, -1).astype(F32))


def _moe(h, hn_slabs, route, w_gate, w_up, w_down, post, tile_m, tile_f):
    N, D = h.shape
    E = w_gate.shape[0]
    n_slots = 2 * N
    R = n_slots + E * tile_m
    n_tiles = R // tile_m

    experts = jnp.concatenate([route[:, 2], route[:, 3]]).astype(jnp.int32)
    onehot = (experts[:, None] == jnp.arange(E, dtype=jnp.int32)[None, :]).astype(jnp.int32)
    ranks = jnp.cumsum(onehot, axis=0) - onehot
    counts = jnp.sum(onehot, axis=0)
    padded = ((counts + tile_m - 1) // tile_m) * tile_m
    ends = jnp.cumsum(padded)
    pos = jnp.sum(onehot * (ranks + (ends - padded)[None, :]), axis=1)
    gap_begin = jnp.concatenate([ends - padded + counts, ends[E - 1:]])
    gap_size = jnp.concatenate([padded - counts, R - ends[E - 1:]])
    gap_end_idx = jnp.cumsum(gap_size)
    j = jnp.arange(R - n_slots, dtype=jnp.int32)
    seg = jnp.sum((j[:, None] >= gap_end_idx[None, :]).astype(jnp.int32), axis=1)
    seg_hot = (seg[:, None] == jnp.arange(E + 1, dtype=jnp.int32)[None, :]).astype(jnp.int32)
    fill_rows = j + jnp.sum(seg_hot * (gap_begin - (gap_end_idx - gap_size))[None, :], axis=1)
    row_of_slot = jnp.concatenate([pos, fill_rows]).astype(jnp.int32)
    tile_start = jnp.arange(n_tiles, dtype=jnp.int32) * tile_m
    tile_expert = jnp.minimum(
        jnp.sum((ends[None, :] <= tile_start[:, None]).astype(jnp.int32), axis=1), E - 1)
    n_valid = (ends[E - 1] // tile_m).reshape(1).astype(jnp.int32)

    x_sorted = _scatter_rows(hn_slabs, row_of_slot, tile_m)
    y_sorted = _moe_experts(x_sorted, tile_expert, n_valid, w_gate, w_up, w_down, tile_m, tile_f)
    ys = _gather_rows(y_sorted, pos, tile_m)
    return _moe_combine(h, ys, route, post, tile_m)


MIX_TILE = 256
RWKV_TILE = 512
SB_BLOCK = 256
FFN_TILE_M = 1024
FFN_TILE_F = 896
MOE_TILE_M = 512
MOE_TILE_F = 896


def _layer(x, p, v_first):
    B, T, D = x.shape
    r, lw, k, v, kk, a, gate, bonus, q, sk, sv = _mix_in(x, p, v_first, MIX_TILE)
    y = _rwkv_scan(r, lw, k, v, kk, a, RWKV_TILE)
    o = _sb_attention(q, sk, sv, SB_BLOCK)
    h, hn, route = _mix_out(y, bonus, gate, o, x, p, MIX_TILE)
    h2 = h.reshape(B * T, D)
    if 'router' in p:
        out = _moe(h2, hn, route.reshape(B * T, LANES), p['moe_gate'], p['moe_up'], p['moe_down'],
                   p['ffn_post'], MOE_TILE_M, MOE_TILE_F)
    else:
        out = _ffn(h2, hn, p['ff_gate'], p['ff_up'], p['ff_down'], p['ffn_post'],
                   FFN_TILE_M, FFN_TILE_F)
    return out.reshape(B, T, D), (v if v_first is None else v_first)


def kernel(x, mix_pre_0, w_in_0, mu_0, decay0_0, decay_up_0, iclr0_0, iclr_up_0, gate_up_0, k_k_0, k_a_0, r_k_0, lnx_g_0, lnx_b_0, sb_g_0, w_out_0, mix_post_0, ffn_pre_0, ff_gate_0, ff_up_0, ff_down_0, ffn_post_0, mix_pre_1, w_in_1, mu_1, decay0_1, decay_up_1, iclr0_1, iclr_up_1, gate_up_1, vres0_1, vres_up_1, k_k_1, k_a_1, r_k_1, lnx_g_1, lnx_b_1, sb_g_1, w_out_1, mix_post_1, ffn_pre_1, router_1, moe_gate_1, moe_up_1, moe_down_1, ffn_post_1):
    p0 = dict(mix_pre=mix_pre_0, w_in=w_in_0, mu=mu_0, decay0=decay0_0, decay_up=decay_up_0,
              iclr0=iclr0_0, iclr_up=iclr_up_0, gate_up=gate_up_0, k_k=k_k_0, k_a=k_a_0,
              r_k=r_k_0, lnx_g=lnx_g_0, lnx_b=lnx_b_0, sb_g=sb_g_0, w_out=w_out_0,
              mix_post=mix_post_0, ffn_pre=ffn_pre_0, ff_gate=ff_gate_0, ff_up=ff_up_0,
              ff_down=ff_down_0, ffn_post=ffn_post_0)
    p1 = dict(mix_pre=mix_pre_1, w_in=w_in_1, mu=mu_1, decay0=decay0_1, decay_up=decay_up_1,
              iclr0=iclr0_1, iclr_up=iclr_up_1, gate_up=gate_up_1, vres0=vres0_1,
              vres_up=vres_up_1, k_k=k_k_1, k_a=k_a_1, r_k=r_k_1, lnx_g=lnx_g_1, lnx_b=lnx_b_1,
              sb_g=sb_g_1, w_out=w_out_1, mix_post=mix_post_1, ffn_pre=ffn_pre_1,
              router=router_1, moe_gate=moe_gate_1, moe_up=moe_up_1, moe_down=moe_down_1,
              ffn_post=ffn_post_1)
    x, v_first = _layer(x, p0, None)
    x, _ = _layer(x, p1, v_first)
    return x
```

```python
import functools
import math

import jax
import jax.numpy as jnp
from jax import lax
from jax.experimental import pallas as pl
from jax.experimental.pallas import tpu as pltpu

HEAD_DIM = 64
RWKV_W = 512
SB_W = 512
N_EXPERTS = 8
NORM_EPS = 1e-6
LNX_EPS = 64e-5
DECAY_SCALE = math.exp(-0.5)
LANES = 128
CHUNK = 64
LOW_SLOT = 128
ROW_SLABS = 8
VMEM_LIMIT = 56 * 1024 * 1024

F32 = jnp.float32
BF16 = jnp.bfloat16


def _dot(a, b):
    return jnp.dot(a, b, preferred_element_type=F32)


def _dot_nt(a, b):
    return lax.dot_general(a, b, (((1,), (1,)), ((), ())), preferred_element_type=F32)


def _dot_tn(a, b):
    return lax.dot_general(a, b, (((0,), (0,)), ((), ())), preferred_element_type=F32)


def _head_ones(width):
    r = lax.broadcasted_iota(jnp.int32, (width, width), 0) // HEAD_DIM
    c = lax.broadcasted_iota(jnp.int32, (width, width), 1) // HEAD_DIM
    return jnp.where(r == c, 1.0, 0.0).astype(BF16)


def _head_sum(x, ones_bd):
    hi = x.astype(BF16)
    lo = (x - hi.astype(F32)).astype(BF16)
    return _dot(hi, ones_bd) + _dot(lo, ones_bd)


def _mix_in_kernel(has_vres, x_ref, g_ref, w_ref, mu_ref, decay0_ref, decay_up_ref,
                   iclr0_ref, iclr_up_ref, gate_up_ref, vres0_ref, vres_up_ref,
                   kk_g_ref, ka_ref, rk_ref, vfirst_ref,
                   r_o, lw_o, k_o, v_o, kk_o, a_o, gate_o, bonus_o, q_o, sk_o, sv_o,
                   carry_ref):
    t = pl.program_id(1)
    n_shift = 3 * RWKV_W + 4 * LOW_SLOT

    @pl.when(t == 0)
    def _():
        carry_ref[...] = jnp.zeros_like(carry_ref)

    x = x_ref[0]
    hn = x * lax.rsqrt(jnp.mean(x * x, axis=-1, keepdims=True) + NORM_EPS) * g_ref[...]
    p = _dot(hn.astype(BF16), w_ref[...])
    pr = p[:, :n_shift]
    rows = pr.shape[0]
    rolled = pltpu.roll(pr, 1, 0)
    first = lax.broadcasted_iota(jnp.int32, pr.shape, 0) == 0
    prev = jnp.where(first, carry_ref[0:1, :], rolled)
    carry_ref[0:1, :] = pr[rows - 1:rows, :]
    s = pr + mu_ref[...] * (prev - pr)

    r = s[:, 0:RWKV_W]
    k = s[:, RWKV_W:2 * RWKV_W]
    v = s[:, 2 * RWKV_W:3 * RWKV_W]
    base = 3 * RWKV_W
    w_dn = s[:, base:base + LOW_SLOT]
    a_dn = s[:, base + LOW_SLOT:base + 2 * LOW_SLOT]
    g_dn = s[:, base + 2 * LOW_SLOT:base + 3 * LOW_SLOT]
    logw = -DECAY_SCALE * jax.nn.sigmoid(
        decay0_ref[...] + _dot(jnp.tanh(w_dn).astype(BF16), decay_up_ref[...]))
    iclr = jax.nn.sigmoid(iclr0_ref[...] + _dot(a_dn.astype(BF16), iclr_up_ref[...]))
    gate = _dot(jax.nn.sigmoid(g_dn).astype(BF16), gate_up_ref[...])
    if has_vres:
        vr_dn = s[:, base + 3 * LOW_SLOT:base + 4 * LOW_SLOT]
        mix = jax.nn.sigmoid(vres0_ref[...] + _dot(vr_dn.astype(BF16), vres_up_ref[...]))
        v = v + (vfirst_ref[0] - v) * mix

    ones_bd = _head_ones(RWKV_W)
    kk = k * kk_g_ref[...]
    kk = kk * lax.rsqrt(jnp.maximum(_head_sum(kk * kk, ones_bd), 1e-12))
    k = k * (1.0 + (iclr - 1.0) * ka_ref[...])
    bonus = _head_sum(r * k * rk_ref[...], ones_bd) * v

    r_o[0] = r
    lw_o[0] = logw
    k_o[0] = k
    v_o[0] = v
    kk_o[0] = kk
    a_o[0] = iclr
    gate_o[0] = gate
    bonus_o[0] = bonus
    q_o[0] = (p[:, n_shift:n_shift + SB_W] * (HEAD_DIM ** -0.5)).astype(BF16)
    sk_o[0] = p[:, n_shift + SB_W:n_shift + 2 * SB_W].astype(BF16)
    sv_o[0] = p[:, n_shift + 2 * SB_W:n_shift + 3 * SB_W].astype(BF16)


def _pad_cols(a, width):
    return jnp.pad(a, ((0, 0), (0, width - a.shape[1])))


def _pad_rows(a, height):
    return jnp.pad(a, ((0, height - a.shape[0]), (0, 0)))


def _mix_in(x, p, v_first, tile):
    B, T, D = x.shape
    has_vres = v_first is not None
    w_in, mu = p['w_in'], p['mu']
    ranks = [32, 32, 96] + ([32] if has_vres else [])
    off = 3 * RWKV_W
    low_w, low_mu = [], []
    for rk in ranks:
        low_w.append(_pad_cols(w_in[:, off:off + rk], LOW_SLOT))
        low_mu.append(jnp.pad(mu[off:off + rk], (0, LOW_SLOT - rk)))
        off += rk
    if not has_vres:
        low_w.append(jnp.zeros((D, LOW_SLOT), F32))
        low_mu.append(jnp.zeros((LOW_SLOT,), F32))
    w_all = jnp.concatenate([w_in[:, :3 * RWKV_W]] + low_w + [w_in[:, off:]], axis=1).astype(BF16)
    mu_all = jnp.concatenate([mu[:3 * RWKV_W]] + low_mu)[None, :]
    n_shift = 3 * RWKV_W + 4 * LOW_SLOT
    n_cols = n_shift + 3 * SB_W
    assert w_all.shape == (D, n_cols)

    row = lambda a: a.reshape(1, -1).astype(F32)
    up = lambda a: _pad_rows(a, LOW_SLOT).astype(BF16)
    if has_vres:
        vres0, vres_up, vf = row(p['vres0']), up(p['vres_up']), v_first
    else:
        vres0 = jnp.zeros((1, RWKV_W), F32)
        vres_up = jnp.zeros((LOW_SLOT, RWKV_W), BF16)
        vf = jnp.zeros((B, 8, RWKV_W), F32)

    grid = (B, T // tile)
    full = lambda shape: pl.BlockSpec(shape, lambda b, t: (0,) * len(shape))
    tok = lambda width: pl.BlockSpec((1, tile, width), lambda b, t: (b, t, 0))
    vf_spec = tok(RWKV_W) if has_vres else pl.BlockSpec((1, 8, RWKV_W), lambda b, t: (b, 0, 0))
    f32_out = jax.ShapeDtypeStruct((B, T, RWKV_W), F32)
    bf_out = jax.ShapeDtypeStruct((B, T, SB_W), BF16)
    outs = pl.pallas_call(
        functools.partial(_mix_in_kernel, has_vres),
        grid=grid,
        in_specs=[tok(D), full((1, D)), full((D, n_cols)), full((1, n_shift)),
                  full((1, RWKV_W)), full((LOW_SLOT, RWKV_W)),
                  full((1, RWKV_W)), full((LOW_SLOT, RWKV_W)),
                  full((LOW_SLOT, RWKV_W)),
                  full((1, RWKV_W)), full((LOW_SLOT, RWKV_W)),
                  full((1, RWKV_W)), full((1, RWKV_W)), full((1, RWKV_W)),
                  vf_spec],
        out_specs=[tok(RWKV_W)] * 8 + [tok(SB_W)] * 3,
        out_shape=[f32_out] * 8 + [bf_out] * 3,
        scratch_shapes=[pltpu.VMEM((8, n_shift), F32)],
        compiler_params=pltpu.CompilerParams(
            dimension_semantics=("arbitrary", "arbitrary"), vmem_limit_bytes=VMEM_LIMIT),
        name="mix_in",
    )(x, row(p['mix_pre']), w_all, mu_all,
      row(p['decay0']), up(p['decay_up']), row(p['iclr0']), up(p['iclr_up']), up(p['gate_up']),
      vres0, vres_up, row(p['k_k']), row(p['k_a']), row(p['r_k']), vf)
    return outs


def _stack(q, m0, m1):
    return jnp.concatenate([q * m0, q * m1], axis=0)


def _rwkv_kernel(n_chunks, r_ref, lw_ref, k_ref, v_ref, kk_ref, a_ref, y_ref,
                 s_ref, rhat_sc, yin_sc, amat_sc, dmat_sc, decay_sc):
    t = pl.program_id(2)
    cur = lax.rem(t, 2)
    prev = 1 - cur
    C = CHUNK

    @pl.when(t == 0)
    def _():
        s_ref[...] = jnp.zeros_like(s_ref)
        rhat_sc[1] = jnp.zeros_like(rhat_sc[1])
        yin_sc[1] = jnp.zeros_like(yin_sc[1])
        amat_sc[1] = jnp.zeros_like(amat_sc[1])
        dmat_sc[1] = jnp.zeros_like(dmat_sc[1])
        decay_sc[1] = jnp.zeros_like(decay_sc[1])

    scan = {'s': s_ref[...], 'done': 0}
    n_slots = 8

    def scan_steps(slot):
        upto = ((slot + 1) * n_chunks) // n_slots
        for c in range(scan['done'], upto):
            s = scan['s']
            s_b = s.astype(BF16)
            y_st = _dot_nt(rhat_sc[prev, c], s_b)
            y_ref[0, c * C:(c + 1) * C, :] = y_st[:C, :] + y_st[C:, :] + yin_sc[prev, c]
            scan['s'] = (s * decay_sc[prev, c][0:1, :] + _dot(s_b, amat_sc[prev, c])
                         + dmat_sc[prev, c])
        scan['done'] = upto

    C2 = 2 * C
    lane = lax.broadcasted_iota(jnp.int32, (1, LANES), 1)
    m0 = jnp.where(lane < HEAD_DIM, 1.0, 0.0)
    m1 = 1.0 - m0
    ti = lax.broadcasted_iota(jnp.int32, (C, C), 0)
    tj = lax.broadcasted_iota(jnp.int32, (C, C), 1)
    tri = jnp.where(tj <= ti, 1.0, 0.0).astype(BF16)
    r2 = lax.broadcasted_iota(jnp.int32, (C2, C2), 0)
    c2 = lax.broadcasted_iota(jnp.int32, (C2, C2), 1)
    same_head = (r2 // C) == (c2 // C)
    strict = jnp.logical_and(same_head, c2 < r2)
    incl = jnp.logical_and(same_head, c2 <= r2)
    eye = jnp.where(r2 == c2, 1.0, 0.0)

    def bf(a):
        return a.astype(BF16)

    r = r_ref[0]
    lw = lw_ref[0]
    k = k_ref[0]
    v = v_ref[0]
    kk = kk_ref[0]
    b = kk * a_ref[0]

    lw_hi = bf(lw)
    lw_mid = bf(lw - lw_hi.astype(F32))
    lw_lo = bf(lw - lw_hi.astype(F32) - lw_mid.astype(F32))
    cums = []
    for c in range(n_chunks):
        sl = slice(c * C, (c + 1) * C)
        cums.append(_dot(tri, lw_hi[sl]) + (_dot(tri, lw_mid[sl]) + _dot(tri, lw_lo[sl])))
    cum = jnp.concatenate(cums, axis=0)
    total = jnp.concatenate(
        [jnp.broadcast_to(cc[C - 1:C, :], (C, LANES)) for cc in cums], axis=0)
    scan_steps(0)
    e_neg = jnp.exp(-cum)
    e_tail = jnp.exp(total - cum)
    rt_all = r * jnp.exp(cum)
    zt_all = -kk * jnp.exp(cum - lw)
    bt_all = b * e_neg
    kt_all = k * e_neg
    bh_all = b * e_tail
    kh_all = k * e_tail
    decay_all = jnp.exp(total)

    chunks = range(n_chunks)
    cut = lambda x, c: _stack(x[c * C:(c + 1) * C, :], m0, m1)
    rt = [cut(rt_all, c) for c in chunks]
    zt = [cut(zt_all, c) for c in chunks]
    bh = [bf(cut(bh_all, c)) for c in chunks]
    kh = [bf(cut(kh_all, c)) for c in chunks]
    vs = [bf(cut(v, c)) for c in chunks]
    g = [_dot_nt(bf(jnp.concatenate([zt[c], rt[c]], axis=0)),
                 bf(jnp.concatenate([cut(bt_all, c), cut(kt_all, c)], axis=0))) for c in chunks]
    scan_steps(1)
    l_zb = [jnp.where(strict, g[c][:C2, :C2], 0.0) for c in chunks]
    l_zk = [bf(jnp.where(strict, g[c][:C2, C2:], 0.0)) for c in chunks]
    l_rb = [bf(jnp.where(incl, g[c][C2:, :C2], 0.0)) for c in chunks]
    l_rk = [bf(jnp.where(incl, g[c][C2:, C2:], 0.0)) for c in chunks]

    inv = [eye + l_zb[c] for c in chunks]
    pw = l_zb
    for i in range(int(math.log2(C)) - 1):
        pw_b = [bf(p) for p in pw]
        pw = [_dot(p, p) for p in pw_b]
        inv = [inv[c] + _dot(bf(inv[c]), bf(pw[c])) for c in chunks]
        scan_steps(2 + i)

    zk_v = [_dot(l_zk[c], vs[c]) for c in chunks]
    zp = [_dot(bf(inv[c]), bf(jnp.concatenate([zt[c], zk_v[c]], axis=1))) for c in chunks]
    scan_steps(n_slots - 1)
    s_ref[...] = scan['s']
    zp_b = [bf(x) for x in zp]
    rk_v = [_dot(l_rk[c], vs[c]) for c in chunks]
    ry = [_dot(l_rb[c], zp_b[c]) for c in chunks]
    r_hat = [bf(rt[c] + ry[c][:, :LANES]) for c in chunks]
    y_in = [rk_v[c] + ry[c][:, LANES:] for c in chunks]
    a_mat = [bf(_dot_tn(zp_b[c][:, :LANES], bh[c])) for c in chunks]
    d_mat = [_dot_tn(jnp.concatenate([zp_b[c][:, LANES:], vs[c]], axis=0),
                     jnp.concatenate([bh[c], kh[c]], axis=0)) for c in chunks]

    for c in chunks:
        rhat_sc[cur, c] = r_hat[c]
        yin_sc[cur, c] = y_in[c][:C, :] + y_in[c][C:, :]
        amat_sc[cur, c] = a_mat[c]
        dmat_sc[cur, c] = d_mat[c]
        decay_sc[cur, c] = decay_all[c * C:c * C + 8, :]


def _rwkv_scan(r, lw, k, v, kk, a, tile):
    B, T, W = r.shape
    n_pairs = W // LANES
    n_tiles = T // tile
    n_chunks = tile // CHUNK
    in_spec = pl.BlockSpec((1, tile, LANES), lambda b, h, t: (b, jnp.minimum(t, n_tiles - 1), h))
    out_spec = pl.BlockSpec((1, tile, LANES), lambda b, h, t: (b, jnp.maximum(t - 1, 0), h))
    return pl.pallas_call(
        functools.partial(_rwkv_kernel, n_chunks),
        grid=(B, n_pairs, n_tiles + 1),
        in_specs=[in_spec] * 6,
        out_specs=out_spec,
        out_shape=jax.ShapeDtypeStruct((B, T, W), F32),
        scratch_shapes=[pltpu.VMEM((LANES, LANES), F32),
                        pltpu.VMEM((2, n_chunks, 2 * CHUNK, LANES), BF16),
                        pltpu.VMEM((2, n_chunks, CHUNK, LANES), F32),
                        pltpu.VMEM((2, n_chunks, LANES, LANES), BF16),
                        pltpu.VMEM((2, n_chunks, LANES, LANES), F32),
                        pltpu.VMEM((2, n_chunks, 8, LANES), F32)],
        compiler_params=pltpu.CompilerParams(
            dimension_semantics=("arbitrary", "arbitrary", "arbitrary"),
            vmem_limit_bytes=VMEM_LIMIT),
        name="rwkv_scan",
    )(r, lw, k, v, kk, a)


SB_LOG_ZERO = -104.0
SB_ROW_PARTS = 2


def _sb_kernel(blk, q_ref, k_ref, v_ref, o_ref, acc_ref, carry_ref):
    lane = lax.broadcasted_iota(jnp.int32, (1, LANES), 1)
    in_h0 = lane < HEAD_DIM
    row = lax.broadcasted_iota(jnp.int32, (blk, blk), 0)
    col = lax.broadcasted_iota(jnp.int32, (blk, blk), 1)
    causal = col < row
    tri = jnp.where(row > col, 1.0, 0.0).astype(BF16)

    tri_sum = jnp.concatenate([tri, jnp.ones((blk, LANES), BF16)], axis=1)
    n_parts = SB_ROW_PARTS
    part = blk // n_parts
    chains = [(h, p) for h in (0, 1) for p in range(n_parts)]
    rows_of = lambda p: slice(p * part, (p + 1) * part)
    reps = blk // LANES

    def one_block(q_heads, j, masked):
        start = pl.multiple_of(j * blk, blk)
        kj = k_ref[0, pl.ds(start, blk), :]
        vj = v_ref[0, pl.ds(start, blk), :]
        z = [_dot_nt(q_heads[h][rows_of(p)], kj) for h, p in chains]
        sp = [jnp.maximum(x, 0.0) + jnp.log(1.0 + jnp.exp(-jnp.abs(x))) for x in z]
        lk = [-x for x in sp]
        if masked:
            lk = [jnp.where(causal[rows_of(p)], x, 0.0) for (h, p), x in zip(chains, lk)]
        sums = [_dot(x.astype(BF16), tri_sum) for x in lk]
        carry = [carry_ref[h, rows_of(p), :] for h, p in chains]
        att = [jnp.exp(z[c] - sp[c] + sums[c][:, :blk] + jnp.concatenate([carry[c]] * reps, axis=1))
               for c in range(len(chains))]
        if masked:
            att = [jnp.where(causal[rows_of(p)], x, 0.0) for (h, p), x in zip(chains, att)]
        new_carry = [carry[c] + sums[c][:, blk:] for c in range(len(chains))]
        for c, (h, p) in enumerate(chains):
            acc_ref[h, rows_of(p), :] += _dot(att[c].astype(BF16), vj)
            carry_ref[h, rows_of(p), :] = new_carry[c]
        top = new_carry[0]
        for x in new_carry[1:]:
            top = jnp.maximum(top, x)
        return jnp.max(top)

    def cond(state):
        j, top = state
        return jnp.logical_and(j >= 0, top > SB_LOG_ZERO)

    def query_block(qi, unused):
        q_start = pl.multiple_of(qi * blk, blk)
        q = q_ref[0, pl.ds(q_start, blk), :]
        zero = jnp.zeros_like(q)
        q_heads = (jnp.where(in_h0, q, zero), jnp.where(in_h0, zero, q))
        acc_ref[...] = jnp.zeros_like(acc_ref)
        carry_ref[...] = jnp.zeros_like(carry_ref)
        top = one_block(q_heads, qi, True)

        def body(state):
            j, _ = state
            return j - 1, one_block(q_heads, j, False)

        lax.while_loop(cond, body, (qi - 1, top))
        o_ref[0, pl.ds(q_start, blk), :] = jnp.where(in_h0, acc_ref[0], acc_ref[1])
        return unused

    lax.fori_loop(0, q_ref.shape[1] // blk, query_block, 0)


def _sb_attention(q, k, v, blk):
    B, T, W = q.shape
    n_pairs = W // LANES
    seq_spec = pl.BlockSpec((1, T, LANES), lambda b, h: (b, 0, h))
    return pl.pallas_call(
        functools.partial(_sb_kernel, blk),
        grid=(B, n_pairs),
        in_specs=[seq_spec, seq_spec, seq_spec],
        out_specs=seq_spec,
        out_shape=jax.ShapeDtypeStruct((B, T, W), F32),
        scratch_shapes=[pltpu.VMEM((2, blk, LANES), F32), pltpu.VMEM((2, blk, LANES), F32)],
        compiler_params=pltpu.CompilerParams(
            dimension_semantics=("arbitrary", "arbitrary"),
            vmem_limit_bytes=VMEM_LIMIT),
        name="sb_attention",
    )(q, k, v)


def _rms(x, g):
    return x * lax.rsqrt(jnp.mean(x * x, axis=-1, keepdims=True) + NORM_EPS) * g


def _split2(x):
    hi = x.astype(BF16)
    return hi, (x - hi.astype(F32)).astype(BF16)


def _mix_out_kernel(has_router, y_ref, bonus_ref, gate_ref, o_ref, x_ref,
                    lnx_g_ref, lnx_b_ref, sb_g_ref, w_out_ref, post_ref, pre_ref, router_ref,
                    h_o, hn_o, route_o):
    ones_bd = _head_ones(RWKV_W)
    inv_n = 1.0 / HEAD_DIM
    y = y_ref[0]
    mean = _head_sum(y, ones_bd) * inv_n
    yc = y - mean
    var = _head_sum(yc * yc, ones_bd) * inv_n
    ya = (yc * lax.rsqrt(var + LNX_EPS) * lnx_g_ref[...] + lnx_b_ref[...] + bonus_ref[0]) * gate_ref[0]
    o = o_ref[0]
    yb = o * lax.rsqrt(_head_sum(o * o, ones_bd) * inv_n + NORM_EPS) * sb_g_ref[...]
    cat = jnp.concatenate([ya, yb], axis=-1).astype(BF16)
    y2 = _dot(cat, w_out_ref[...])
    h = x_ref[0] + _rms(y2, post_ref[...])
    hn = _rms(h, pre_ref[...])
    h_o[0] = h
    if has_router:
        tile = hn.shape[0]
        for s in range(ROW_SLABS):
            hn_o[pl.ds(s, tile, stride=ROW_SLABS), :] = hn[:, s * LANES:(s + 1) * LANES]
        hi, lo = _split2(hn)
        r_hi, r_lo = _split2(router_ref[...])
        logits = _dot(hi, r_hi) + (_dot(lo, r_hi) + _dot(hi, r_lo))
        lane = lax.broadcasted_iota(jnp.int32, logits.shape, 1)
        neg = jnp.float32(-jnp.inf)
        logits = jnp.where(lane < N_EXPERTS, logits, neg)
        m1 = jnp.max(logits, axis=-1, keepdims=True)
        i1 = jnp.min(jnp.where(logits == m1, lane, LANES), axis=-1, keepdims=True)
        rest = jnp.where(lane == i1, neg, logits)
        m2 = jnp.max(rest, axis=-1, keepdims=True)
        i2 = jnp.min(jnp.where(rest == m2, lane, LANES), axis=-1, keepdims=True)
        e2 = jnp.exp(m2 - m1)
        g1 = 1.0 / (1.0 + e2)
        route = jnp.where(lane == 0, g1, jnp.where(lane == 1, e2 * g1, 0.0))
        route = jnp.where(lane == 2, i1.astype(F32), jnp.where(lane == 3, i2.astype(F32), route))
        route_o[0] = route
    else:
        hn_o[...] = hn.astype(BF16)
        route_o[0] = jnp.zeros_like(route_o[0])


def _mix_out(y, bonus, gate, o, x, p, tile):
    B, T, D = x.shape
    has_router = 'router' in p
    row = lambda a: a.reshape(1, -1).astype(F32)
    router = _pad_cols(p['router'], LANES) if has_router else jnp.zeros((D, LANES), F32)
    full = lambda shape: pl.BlockSpec(shape, lambda b, t: (0,) * len(shape))
    tok = lambda width: pl.BlockSpec((1, tile, width), lambda b, t: (b, t, 0))
    n_t = T // tile
    if has_router:
        assert D == ROW_SLABS * LANES
        hn_spec = pl.BlockSpec((tile * ROW_SLABS, LANES), lambda b, t: (b * n_t + t, 0))
        hn_shape = jax.ShapeDtypeStruct((B * T * ROW_SLABS, LANES), F32)
    else:
        hn_spec = pl.BlockSpec((tile, D), lambda b, t: (b * n_t + t, 0))
        hn_shape = jax.ShapeDtypeStruct((B * T, D), BF16)
    return pl.pallas_call(
        functools.partial(_mix_out_kernel, has_router),
        grid=(B, n_t),
        in_specs=[tok(RWKV_W)] * 4 + [tok(D)] +
                 [full((1, RWKV_W))] * 3 + [full((D, D)), full((1, D)), full((1, D)), full((D, LANES))],
        out_specs=[tok(D), hn_spec, tok(LANES)],
        out_shape=[jax.ShapeDtypeStruct((B, T, D), F32), hn_shape,
                   jax.ShapeDtypeStruct((B, T, LANES), F32)],
        compiler_params=pltpu.CompilerParams(
            dimension_semantics=("arbitrary", "arbitrary"), vmem_limit_bytes=VMEM_LIMIT),
        name="mix_out",
    )(y, bonus, gate, o, x, row(p['lnx_g']), row(p['lnx_b']), row(p['sb_g']),
      p['w_out'].astype(BF16), row(p['mix_post']), row(p['ffn_pre']), router)


def _ffn_kernel(h_ref, hn_ref, wg_ref, wu_ref, wd_ref, post_ref, o_ref, acc_ref):
    f = pl.program_id(1)

    @pl.when(f == 0)
    def _():
        acc_ref[...] = jnp.zeros_like(acc_ref)

    hn = hn_ref[...]
    g = _dot(hn, wg_ref[...])
    u = _dot(hn, wu_ref[...])
    act = (g * jax.nn.sigmoid(g) * u).astype(BF16)
    acc_ref[...] += _dot(act, wd_ref[...])

    @pl.when(f == pl.num_programs(1) - 1)
    def _():
        o_ref[...] = h_ref[...] + _rms(acc_ref[...], post_ref[...])


def _ffn(h, hn, w_gate, w_up, w_down, post, tile_m, tile_f):
    N, D = h.shape
    F = w_gate.shape[1]
    return pl.pallas_call(
        _ffn_kernel,
        grid=(N // tile_m, F // tile_f),
        in_specs=[pl.BlockSpec((tile_m, D), lambda i, f: (i, 0)),
                  pl.BlockSpec((tile_m, D), lambda i, f: (i, 0)),
                  pl.BlockSpec((D, tile_f), lambda i, f: (0, f)),
                  pl.BlockSpec((D, tile_f), lambda i, f: (0, f)),
                  pl.BlockSpec((tile_f, D), lambda i, f: (f, 0)),
                  pl.BlockSpec((1, D), lambda i, f: (0, 0))],
        out_specs=pl.BlockSpec((tile_m, D), lambda i, f: (i, 0)),
        out_shape=jax.ShapeDtypeStruct((N, D), F32),
        scratch_shapes=[pltpu.VMEM((tile_m, D), F32)],
        compiler_params=pltpu.CompilerParams(
            dimension_semantics=("arbitrary", "arbitrary"), vmem_limit_bytes=VMEM_LIMIT),
        name="ffn",
    )(h, hn, w_gate.astype(BF16), w_up.astype(BF16), w_down.astype(BF16),
      post.reshape(1, -1).astype(F32))


GATHER_UNROLL = 8


def _gather_rows_kernel(idx_ref, src_ref, out_ref, sem):
    n = out_ref.shape[0]

    def issue(g, carry):
        for u in range(GATHER_UNROLL):
            i = g * GATHER_UNROLL + u
            pltpu.make_async_copy(src_ref.at[idx_ref[0, 0, i]], out_ref.at[i], sem).start(
                priority=u % 2)
        return carry

    lax.fori_loop(0, n // GATHER_UNROLL, issue, 0)
    pltpu.make_async_copy(src_ref.at[pl.ds(0, n)], out_ref, sem).wait()


def _gather_rows(src, idx, tile):
    n_src = src.shape[0] // ROW_SLABS
    n_out = idx.shape[0]
    n_tiles = n_out // tile
    out = pl.pallas_call(
        _gather_rows_kernel,
        grid=(n_tiles,),
        in_specs=[pl.BlockSpec((1, 1, tile), lambda i: (i, 0, 0), memory_space=pltpu.SMEM),
                  pl.BlockSpec(memory_space=pl.ANY)],
        out_specs=pl.BlockSpec((tile, ROW_SLABS, LANES), lambda i: (i, 0, 0)),
        out_shape=jax.ShapeDtypeStruct((n_out, ROW_SLABS, LANES), F32),
        scratch_shapes=[pltpu.SemaphoreType.DMA(())],
        compiler_params=pltpu.CompilerParams(
            dimension_semantics=("arbitrary",), vmem_limit_bytes=VMEM_LIMIT),
        name="gather_rows",
    )(idx.reshape(n_tiles, 1, tile), src.reshape(n_src, ROW_SLABS, LANES))
    return out.reshape(n_out * ROW_SLABS, LANES)


def _scatter_rows_kernel(idx_ref, src_ref, out_ref, sem):
    n = src_ref.shape[0]

    def issue(g, carry):
        for u in range(GATHER_UNROLL):
            i = g * GATHER_UNROLL + u
            pltpu.make_async_copy(src_ref.at[i], out_ref.at[idx_ref[0, 0, i]], sem).start(
                priority=u % 2)
        return carry

    lax.fori_loop(0, n // GATHER_UNROLL, issue, 0)
    pltpu.make_async_copy(src_ref, out_ref.at[pl.ds(0, n)], sem).wait()


def _scatter_rows(src, idx, tile):
    n_src = src.shape[0] // ROW_SLABS
    n_out = idx.shape[0]
    n_tiles = n_out // tile
    src_tiles = n_src // tile
    out = pl.pallas_call(
        _scatter_rows_kernel,
        grid=(n_tiles,),
        in_specs=[pl.BlockSpec((1, 1, tile), lambda i: (i, 0, 0), memory_space=pltpu.SMEM),
                  pl.BlockSpec((tile, ROW_SLABS, LANES), lambda i: (lax.rem(i, src_tiles), 0, 0))],
        out_specs=pl.BlockSpec(memory_space=pl.ANY),
        out_shape=jax.ShapeDtypeStruct((n_out, ROW_SLABS, LANES), F32),
        scratch_shapes=[pltpu.SemaphoreType.DMA(())],
        compiler_params=pltpu.CompilerParams(
            dimension_semantics=("arbitrary",), vmem_limit_bytes=VMEM_LIMIT),
        name="scatter_rows",
    )(idx.reshape(n_tiles, 1, tile), src.reshape(n_src, ROW_SLABS, LANES))
    return out.reshape(n_out * ROW_SLABS, LANES)


def _unslab(ref, rows):
    return jnp.concatenate(
        [ref[pl.ds(s, rows, stride=ROW_SLABS), :] for s in range(ROW_SLABS)], axis=-1)


def _moe_kernel(te_ref, nv_ref, x_ref, wg_ref, wu_ref, wd_ref, y_ref, acc_ref, xb_ref):
    i = pl.program_id(0)
    f = pl.program_id(1)
    rows = xb_ref.shape[0]
    valid = i < nv_ref[0]

    @pl.when(jnp.logical_and(valid, f == 0))
    def _():
        acc_ref[...] = jnp.zeros_like(acc_ref)
        xb_ref[...] = _unslab(x_ref, rows).astype(BF16)

    @pl.when(valid)
    def _():
        xb = xb_ref[...]
        g = _dot(xb, wg_ref[0])
        u = _dot(xb, wu_ref[0])
        act = (g * jax.nn.sigmoid(g) * u).astype(BF16)
        acc_ref[...] += _dot(act, wd_ref[0])

    @pl.when(f == pl.num_programs(1) - 1)
    def _():
        acc = jnp.where(valid, acc_ref[...], 0.0)
        for s in range(ROW_SLABS):
            y_ref[pl.ds(s, rows, stride=ROW_SLABS), :] = acc[:, s * LANES:(s + 1) * LANES]


def _moe_experts(x_sorted, tile_expert, n_valid, w_gate, w_up, w_down, tile_m, tile_f):
    E, D, F = w_gate.shape
    R = x_sorted.shape[0] // ROW_SLABS
    n_f = F // tile_f

    def w_col(i, f, te, nv):
        return (te[i], 0, jnp.where(i < nv[0], f, n_f - 1))

    def w_row(i, f, te, nv):
        return (te[i], jnp.where(i < nv[0], f, n_f - 1), 0)

    slab = pl.BlockSpec((tile_m * ROW_SLABS, LANES), lambda i, f, te, nv: (i, 0))
    return pl.pallas_call(
        _moe_kernel,
        grid_spec=pltpu.PrefetchScalarGridSpec(
            num_scalar_prefetch=2,
            grid=(R // tile_m, n_f),
            in_specs=[slab,
                      pl.BlockSpec((1, D, tile_f), w_col),
                      pl.BlockSpec((1, D, tile_f), w_col),
                      pl.BlockSpec((1, tile_f, D), w_row)],
            out_specs=slab,
            scratch_shapes=[pltpu.VMEM((tile_m, D), F32), pltpu.VMEM((tile_m, D), BF16)]),
        out_shape=jax.ShapeDtypeStruct((R * ROW_SLABS, LANES), F32),
        compiler_params=pltpu.CompilerParams(
            dimension_semantics=("arbitrary", "arbitrary"), vmem_limit_bytes=VMEM_LIMIT),
        name="moe_experts",
    )(tile_expert, n_valid, x_sorted, w_gate.astype(BF16), w_up.astype(BF16), w_down.astype(BF16))


def _moe_combine_kernel(h_ref, y1_ref, y2_ref, route_ref, post_ref, o_ref):
    rows = h_ref.shape[0]
    route = route_ref[...]
    f = route[:, 0:1] * _unslab(y1_ref, rows) + route[:, 1:2] * _unslab(y2_ref, rows)
    o_ref[...] = h_ref[...] + _rms(f, post_ref[...])


def _moe_combine(h, ys, route, post, tile):
    N, D = h.shape
    n_t = N // tile
    return pl.pallas_call(
        _moe_combine_kernel,
        grid=(n_t,),
        in_specs=[pl.BlockSpec((tile, D), lambda i: (i, 0)),
                  pl.BlockSpec((tile * ROW_SLABS, LANES), lambda i: (i, 0)),
                  pl.BlockSpec((tile * ROW_SLABS, LANES), lambda i: (n_t + i, 0)),
                  pl.BlockSpec((tile, LANES), lambda i: (i, 0)),
                  pl.BlockSpec((1, D), lambda i: (0, 0))],
        out_specs=pl.BlockSpec((tile, D), lambda i: (i, 0)),
        out_shape=jax.ShapeDtypeStruct((N, D), F32),
        compiler_params=pltpu.CompilerParams(
            dimension_semantics=("arbitrary",), vmem_limit_bytes=VMEM_LIMIT),
        name="moe_combine",
    )(h, ys, ys, route, post.reshape(1, -1).astype(F32))


def _moe(h, hn_slabs, route, w_gate, w_up, w_down, post, tile_m, tile_f):
    N, D = h.shape
    E = w_gate.shape[0]
    n_slots = 2 * N
    R = n_slots + E * tile_m
    n_tiles = R // tile_m

    experts = jnp.concatenate([route[:, 2], route[:, 3]]).astype(jnp.int32)
    onehot = (experts[:, None] == jnp.arange(E, dtype=jnp.int32)[None, :]).astype(jnp.int32)
    ranks = jnp.cumsum(onehot, axis=0) - onehot
    counts = jnp.sum(onehot, axis=0)
    padded = ((counts + tile_m - 1) // tile_m) * tile_m
    ends = jnp.cumsum(padded)
    pos = jnp.sum(onehot * (ranks + (ends - padded)[None, :]), axis=1)
    gap_begin = jnp.concatenate([ends - padded + counts, ends[E - 1:]])
    gap_size = jnp.concatenate([padded - counts, R - ends[E - 1:]])
    gap_end_idx = jnp.cumsum(gap_size)
    j = jnp.arange(R - n_slots, dtype=jnp.int32)
    seg = jnp.sum((j[:, None] >= gap_end_idx[None, :]).astype(jnp.int32), axis=1)
    seg_hot = (seg[:, None] == jnp.arange(E + 1, dtype=jnp.int32)[None, :]).astype(jnp.int32)
    fill_rows = j + jnp.sum(seg_hot * (gap_begin - (gap_end_idx - gap_size))[None, :], axis=1)
    row_of_slot = jnp.concatenate([pos, fill_rows]).astype(jnp.int32)
    tile_start = jnp.arange(n_tiles, dtype=jnp.int32) * tile_m
    tile_expert = jnp.minimum(
        jnp.sum((ends[None, :] <= tile_start[:, None]).astype(jnp.int32), axis=1), E - 1)
    n_valid = (ends[E - 1] // tile_m).reshape(1).astype(jnp.int32)

    x_sorted = _scatter_rows(hn_slabs, row_of_slot, tile_m)
    y_sorted = _moe_experts(x_sorted, tile_expert, n_valid, w_gate, w_up, w_down, tile_m, tile_f)
    ys = _gather_rows(y_sorted, pos, tile_m)
    return _moe_combine(h, ys, route, post, tile_m)


MIX_TILE = 512
RWKV_TILE = 512
SB_BLOCK = 256
FFN_TILE_M = 1024
FFN_TILE_F = 896
MOE_TILE_M = 512
MOE_TILE_F = 896


def _layer(x, p, v_first):
    B, T, D = x.shape
    r, lw, k, v, kk, a, gate, bonus, q, sk, sv = _mix_in(x, p, v_first, MIX_TILE)
    y = _rwkv_scan(r, lw, k, v, kk, a, RWKV_TILE)
    o = _sb_attention(q, sk, sv, SB_BLOCK)
    h, hn, route = _mix_out(y, bonus, gate, o, x, p, MIX_TILE)
    h2 = h.reshape(B * T, D)
    if 'router' in p:
        out = _moe(h2, hn, route.reshape(B * T, LANES), p['moe_gate'], p['moe_up'], p['moe_down'],
                   p['ffn_post'], MOE_TILE_M, MOE_TILE_F)
    else:
        out = _ffn(h2, hn, p['ff_gate'], p['ff_up'], p['ff_down'], p['ffn_post'],
                   FFN_TILE_M, FFN_TILE_F)
    return out.reshape(B, T, D), (v if v_first is None else v_first)


def kernel(x, mix_pre_0, w_in_0, mu_0, decay0_0, decay_up_0, iclr0_0, iclr_up_0, gate_up_0, k_k_0, k_a_0, r_k_0, lnx_g_0, lnx_b_0, sb_g_0, w_out_0, mix_post_0, ffn_pre_0, ff_gate_0, ff_up_0, ff_down_0, ffn_post_0, mix_pre_1, w_in_1, mu_1, decay0_1, decay_up_1, iclr0_1, iclr_up_1, gate_up_1, vres0_1, vres_up_1, k_k_1, k_a_1, r_k_1, lnx_g_1, lnx_b_1, sb_g_1, w_out_1, mix_post_1, ffn_pre_1, router_1, moe_gate_1, moe_up_1, moe_down_1, ffn_post_1):
    p0 = dict(mix_pre=mix_pre_0, w_in=w_in_0, mu=mu_0, decay0=decay0_0, decay_up=decay_up_0,
              iclr0=iclr0_0, iclr_up=iclr_up_0, gate_up=gate_up_0, k_k=k_k_0, k_a=k_a_0,
              r_k=r_k_0, lnx_g=lnx_g_0, lnx_b=lnx_b_0, sb_g=sb_g_0, w_out=w_out_0,
              mix_post=mix_post_0, ffn_pre=ffn_pre_0, ff_gate=ff_gate_0, ff_up=ff_up_0,
              ff_down=ff_down_0, ffn_post=ffn_post_0)
    p1 = dict(mix_pre=mix_pre_1, w_in=w_in_1, mu=mu_1, decay0=decay0_1, decay_up=decay_up_1,
              iclr0=iclr0_1, iclr_up=iclr_up_1, gate_up=gate_up_1, vres0=vres0_1,
              vres_up=vres_up_1, k_k=k_k_1, k_a=k_a_1, r_k=r_k_1, lnx_g=lnx_g_1, lnx_b=lnx_b_1,
              sb_g=sb_g_1, w_out=w_out_1, mix_post=mix_post_1, ffn_pre=ffn_pre_1,
              router=router_1, moe_gate=moe_gate_1, moe_up=moe_up_1, moe_down=moe_down_1,
              ffn_post=ffn_post_1)
    x, v_first = _layer(x, p0, None)
    x, _ = _layer(x, p1, v_first)
    return x
```

```python
import functools
import math

import jax
import jax.numpy as jnp
from jax import lax
from jax.experimental import pallas as pl
from jax.experimental.pallas import tpu as pltpu

HEAD_DIM = 64
RWKV_W = 512
SB_W = 512
N_EXPERTS = 8
NORM_EPS = 1e-6
LNX_EPS = 64e-5
DECAY_SCALE = math.exp(-0.5)
LANES = 128
CHUNK = 64
LOW_SLOT = 128
HEAD_SUM_BLOCK = 128
ROW_SLABS = 8
VMEM_LIMIT = 56 * 1024 * 1024

F32 = jnp.float32
BF16 = jnp.bfloat16


def _dot(a, b):
    return jnp.dot(a, b, preferred_element_type=F32)


def _dot_nt(a, b):
    return lax.dot_general(a, b, (((1,), (1,)), ((), ())), preferred_element_type=F32)


def _dot_tn(a, b):
    return lax.dot_general(a, b, (((0,), (0,)), ((), ())), preferred_element_type=F32)


def _head_ones(width):
    r = lax.broadcasted_iota(jnp.int32, (width, width), 0) // HEAD_DIM
    c = lax.broadcasted_iota(jnp.int32, (width, width), 1) // HEAD_DIM
    return jnp.where(r == c, 1.0, 0.0).astype(BF16)


def _head_sum(x, ones_bd):
    hi = x.astype(BF16)
    lo = (x - hi.astype(F32)).astype(BF16)
    w = ones_bd.shape[0]
    blocks = [_dot(hi[:, c:c + w], ones_bd) + _dot(lo[:, c:c + w], ones_bd)
              for c in range(0, x.shape[1], w)]
    return jnp.concatenate(blocks, axis=1)


def _mix_in_kernel(has_vres, x_ref, g_ref, w_ref, mu_ref, decay0_ref, decay_up_ref,
                   iclr0_ref, iclr_up_ref, gate_up_ref, vres0_ref, vres_up_ref,
                   kk_g_ref, ka_ref, rk_ref, vfirst_ref,
                   r_o, lw_o, k_o, v_o, kk_o, a_o, gate_o, bonus_o, q_o, sk_o, sv_o,
                   carry_ref):
    t = pl.program_id(1)
    n_shift = 3 * RWKV_W + 4 * LOW_SLOT

    @pl.when(t == 0)
    def _():
        carry_ref[...] = jnp.zeros_like(carry_ref)

    x = x_ref[0]
    hn = x * lax.rsqrt(jnp.mean(x * x, axis=-1, keepdims=True) + NORM_EPS) * g_ref[...]
    p = _dot(hn.astype(BF16), w_ref[...])
    pr = p[:, :n_shift]
    rows = pr.shape[0]
    rolled = pltpu.roll(pr, 1, 0)
    first = lax.broadcasted_iota(jnp.int32, pr.shape, 0) == 0
    prev = jnp.where(first, carry_ref[0:1, :], rolled)
    carry_ref[0:1, :] = pr[rows - 1:rows, :]
    s = pr + mu_ref[...] * (prev - pr)

    r = s[:, 0:RWKV_W]
    k = s[:, RWKV_W:2 * RWKV_W]
    v = s[:, 2 * RWKV_W:3 * RWKV_W]
    base = 3 * RWKV_W
    w_dn = s[:, base:base + LOW_SLOT]
    a_dn = s[:, base + LOW_SLOT:base + 2 * LOW_SLOT]
    g_dn = s[:, base + 2 * LOW_SLOT:base + 3 * LOW_SLOT]
    logw = -DECAY_SCALE * jax.nn.sigmoid(
        decay0_ref[...] + _dot(jnp.tanh(w_dn).astype(BF16), decay_up_ref[...]))
    iclr = jax.nn.sigmoid(iclr0_ref[...] + _dot(a_dn.astype(BF16), iclr_up_ref[...]))
    gate = _dot(jax.nn.sigmoid(g_dn).astype(BF16), gate_up_ref[...])
    if has_vres:
        vr_dn = s[:, base + 3 * LOW_SLOT:base + 4 * LOW_SLOT]
        mix = jax.nn.sigmoid(vres0_ref[...] + _dot(vr_dn.astype(BF16), vres_up_ref[...]))
        v = v + (vfirst_ref[0] - v) * mix

    ones_bd = _head_ones(HEAD_SUM_BLOCK)
    kk = k * kk_g_ref[...]
    kk = kk * lax.rsqrt(jnp.maximum(_head_sum(kk * kk, ones_bd), 1e-12))
    k = k * (1.0 + (iclr - 1.0) * ka_ref[...])
    bonus = _head_sum(r * k * rk_ref[...], ones_bd) * v

    r_o[0] = r
    lw_o[0] = logw
    k_o[0] = k
    v_o[0] = v
    kk_o[0] = kk
    a_o[0] = iclr
    gate_o[0] = gate
    bonus_o[0] = bonus
    q_o[0] = (p[:, n_shift:n_shift + SB_W] * (HEAD_DIM ** -0.5)).astype(BF16)
    sk_o[0] = p[:, n_shift + SB_W:n_shift + 2 * SB_W].astype(BF16)
    sv_o[0] = p[:, n_shift + 2 * SB_W:n_shift + 3 * SB_W].astype(BF16)


def _pad_cols(a, width):
    return jnp.pad(a, ((0, 0), (0, width - a.shape[1])))


def _pad_rows(a, height):
    return jnp.pad(a, ((0, height - a.shape[0]), (0, 0)))


def _mix_in(x, p, v_first, tile):
    B, T, D = x.shape
    has_vres = v_first is not None
    w_in, mu = p['w_in'], p['mu']
    ranks = [32, 32, 96] + ([32] if has_vres else [])
    off = 3 * RWKV_W
    low_w, low_mu = [], []
    for rk in ranks:
        low_w.append(_pad_cols(w_in[:, off:off + rk], LOW_SLOT))
        low_mu.append(jnp.pad(mu[off:off + rk], (0, LOW_SLOT - rk)))
        off += rk
    if not has_vres:
        low_w.append(jnp.zeros((D, LOW_SLOT), F32))
        low_mu.append(jnp.zeros((LOW_SLOT,), F32))
    w_all = jnp.concatenate([w_in[:, :3 * RWKV_W]] + low_w + [w_in[:, off:]], axis=1).astype(BF16)
    mu_all = jnp.concatenate([mu[:3 * RWKV_W]] + low_mu)[None, :]
    n_shift = 3 * RWKV_W + 4 * LOW_SLOT
    n_cols = n_shift + 3 * SB_W
    assert w_all.shape == (D, n_cols)

    row = lambda a: a.reshape(1, -1).astype(F32)
    up = lambda a: _pad_rows(a, LOW_SLOT).astype(BF16)
    if has_vres:
        vres0, vres_up, vf = row(p['vres0']), up(p['vres_up']), v_first
    else:
        vres0 = jnp.zeros((1, RWKV_W), F32)
        vres_up = jnp.zeros((LOW_SLOT, RWKV_W), BF16)
        vf = jnp.zeros((B, 8, RWKV_W), F32)

    grid = (B, T // tile)
    full = lambda shape: pl.BlockSpec(shape, lambda b, t: (0,) * len(shape))
    tok = lambda width: pl.BlockSpec((1, tile, width), lambda b, t: (b, t, 0))
    vf_spec = tok(RWKV_W) if has_vres else pl.BlockSpec((1, 8, RWKV_W), lambda b, t: (b, 0, 0))
    f32_out = jax.ShapeDtypeStruct((B, T, RWKV_W), F32)
    bf_out = jax.ShapeDtypeStruct((B, T, SB_W), BF16)
    outs = pl.pallas_call(
        functools.partial(_mix_in_kernel, has_vres),
        grid=grid,
        in_specs=[tok(D), full((1, D)), full((D, n_cols)), full((1, n_shift)),
                  full((1, RWKV_W)), full((LOW_SLOT, RWKV_W)),
                  full((1, RWKV_W)), full((LOW_SLOT, RWKV_W)),
                  full((LOW_SLOT, RWKV_W)),
                  full((1, RWKV_W)), full((LOW_SLOT, RWKV_W)),
                  full((1, RWKV_W)), full((1, RWKV_W)), full((1, RWKV_W)),
                  vf_spec],
        out_specs=[tok(RWKV_W)] * 8 + [tok(SB_W)] * 3,
        out_shape=[f32_out] * 8 + [bf_out] * 3,
        scratch_shapes=[pltpu.VMEM((8, n_shift), F32)],
        compiler_params=pltpu.CompilerParams(
            dimension_semantics=("arbitrary", "arbitrary"), vmem_limit_bytes=VMEM_LIMIT),
        name="mix_in",
    )(x, row(p['mix_pre']), w_all, mu_all,
      row(p['decay0']), up(p['decay_up']), row(p['iclr0']), up(p['iclr_up']), up(p['gate_up']),
      vres0, vres_up, row(p['k_k']), row(p['k_a']), row(p['r_k']), vf)
    return outs


def _stack(q, m0, m1):
    return jnp.concatenate([q * m0, q * m1], axis=0)


def _rwkv_kernel(n_chunks, r_ref, lw_ref, k_ref, v_ref, kk_ref, a_ref, y_ref,
                 s_ref, rhat_sc, yin_sc, amat_sc, dmat_sc, decay_sc):
    t = pl.program_id(2)
    cur = lax.rem(t, 2)
    prev = 1 - cur
    C = CHUNK

    @pl.when(t == 0)
    def _():
        s_ref[...] = jnp.zeros_like(s_ref)
        rhat_sc[1] = jnp.zeros_like(rhat_sc[1])
        yin_sc[1] = jnp.zeros_like(yin_sc[1])
        amat_sc[1] = jnp.zeros_like(amat_sc[1])
        dmat_sc[1] = jnp.zeros_like(dmat_sc[1])
        decay_sc[1] = jnp.zeros_like(decay_sc[1])

    scan = {'s': s_ref[...], 'done': 0}
    n_slots = 8

    def scan_steps(slot):
        upto = ((slot + 1) * n_chunks) // n_slots
        for c in range(scan['done'], upto):
            s = scan['s']
            s_b = s.astype(BF16)
            y_st = _dot_nt(rhat_sc[prev, c], s_b)
            y_ref[0, c * C:(c + 1) * C, :] = y_st[:C, :] + y_st[C:, :] + yin_sc[prev, c]
            scan['s'] = (s * decay_sc[prev, c][0:1, :] + _dot(s_b, amat_sc[prev, c])
                         + dmat_sc[prev, c])
        scan['done'] = upto

    C2 = 2 * C
    lane = lax.broadcasted_iota(jnp.int32, (1, LANES), 1)
    m0 = jnp.where(lane < HEAD_DIM, 1.0, 0.0)
    m1 = 1.0 - m0
    ti = lax.broadcasted_iota(jnp.int32, (C, C), 0)
    tj = lax.broadcasted_iota(jnp.int32, (C, C), 1)
    tri = jnp.where(tj <= ti, 1.0, 0.0).astype(BF16)
    r2 = lax.broadcasted_iota(jnp.int32, (C2, C2), 0)
    c2 = lax.broadcasted_iota(jnp.int32, (C2, C2), 1)
    same_head = (r2 // C) == (c2 // C)
    strict = jnp.logical_and(same_head, c2 < r2)
    incl = jnp.logical_and(same_head, c2 <= r2)
    eye = jnp.where(r2 == c2, 1.0, 0.0)

    def bf(a):
        return a.astype(BF16)

    r = r_ref[0]
    lw = lw_ref[0]
    k = k_ref[0]
    v = v_ref[0]
    kk = kk_ref[0]
    b = kk * a_ref[0]

    lw_hi = bf(lw)
    lw_mid = bf(lw - lw_hi.astype(F32))
    lw_lo = bf(lw - lw_hi.astype(F32) - lw_mid.astype(F32))
    cums = []
    for c in range(n_chunks):
        sl = slice(c * C, (c + 1) * C)
        cums.append(_dot(tri, lw_hi[sl]) + (_dot(tri, lw_mid[sl]) + _dot(tri, lw_lo[sl])))
    cum = jnp.concatenate(cums, axis=0)
    total = jnp.concatenate(
        [jnp.broadcast_to(cc[C - 1:C, :], (C, LANES)) for cc in cums], axis=0)
    scan_steps(0)
    e_neg = jnp.exp(-cum)
    e_tail = jnp.exp(total - cum)
    rt_all = r * jnp.exp(cum)
    zt_all = -kk * jnp.exp(cum - lw)
    bt_all = b * e_neg
    kt_all = k * e_neg
    bh_all = b * e_tail
    kh_all = k * e_tail
    decay_all = jnp.exp(total)

    chunks = range(n_chunks)
    cut = lambda x, c: _stack(x[c * C:(c + 1) * C, :], m0, m1)
    rt = [cut(rt_all, c) for c in chunks]
    zt = [cut(zt_all, c) for c in chunks]
    bh = [bf(cut(bh_all, c)) for c in chunks]
    kh = [bf(cut(kh_all, c)) for c in chunks]
    vs = [bf(cut(v, c)) for c in chunks]
    g = [_dot_nt(bf(jnp.concatenate([zt[c], rt[c]], axis=0)),
                 bf(jnp.concatenate([cut(bt_all, c), cut(kt_all, c)], axis=0))) for c in chunks]
    scan_steps(1)
    l_zb = [jnp.where(strict, g[c][:C2, :C2], 0.0) for c in chunks]
    l_zk = [bf(jnp.where(strict, g[c][:C2, C2:], 0.0)) for c in chunks]
    l_rb = [bf(jnp.where(incl, g[c][C2:, :C2], 0.0)) for c in chunks]
    l_rk = [bf(jnp.where(incl, g[c][C2:, C2:], 0.0)) for c in chunks]

    inv = [eye + l_zb[c] for c in chunks]
    pw = l_zb
    for i in range(int(math.log2(C)) - 1):
        pw_b = [bf(p) for p in pw]
        pw = [_dot(p, p) for p in pw_b]
        inv = [inv[c] + _dot(bf(inv[c]), bf(pw[c])) for c in chunks]
        scan_steps(2 + i)

    zk_v = [_dot(l_zk[c], vs[c]) for c in chunks]
    zp = [_dot(bf(inv[c]), bf(jnp.concatenate([zt[c], zk_v[c]], axis=1))) for c in chunks]
    scan_steps(n_slots - 1)
    s_ref[...] = scan['s']
    zp_b = [bf(x) for x in zp]
    rk_v = [_dot(l_rk[c], vs[c]) for c in chunks]
    ry = [_dot(l_rb[c], zp_b[c]) for c in chunks]
    r_hat = [bf(rt[c] + ry[c][:, :LANES]) for c in chunks]
    y_in = [rk_v[c] + ry[c][:, LANES:] for c in chunks]
    a_mat = [bf(_dot_tn(zp_b[c][:, :LANES], bh[c])) for c in chunks]
    d_mat = [_dot_tn(jnp.concatenate([zp_b[c][:, LANES:], vs[c]], axis=0),
                     jnp.concatenate([bh[c], kh[c]], axis=0)) for c in chunks]

    for c in chunks:
        rhat_sc[cur, c] = r_hat[c]
        yin_sc[cur, c] = y_in[c][:C, :] + y_in[c][C:, :]
        amat_sc[cur, c] = a_mat[c]
        dmat_sc[cur, c] = d_mat[c]
        decay_sc[cur, c] = decay_all[c * C:c * C + 8, :]


def _rwkv_scan(r, lw, k, v, kk, a, tile):
    B, T, W = r.shape
    n_pairs = W // LANES
    n_tiles = T // tile
    n_chunks = tile // CHUNK
    in_spec = pl.BlockSpec((1, tile, LANES), lambda b, h, t: (b, jnp.minimum(t, n_tiles - 1), h))
    out_spec = pl.BlockSpec((1, tile, LANES), lambda b, h, t: (b, jnp.maximum(t - 1, 0), h))
    return pl.pallas_call(
        functools.partial(_rwkv_kernel, n_chunks),
        grid=(B, n_pairs, n_tiles + 1),
        in_specs=[in_spec] * 6,
        out_specs=out_spec,
        out_shape=jax.ShapeDtypeStruct((B, T, W), F32),
        scratch_shapes=[pltpu.VMEM((LANES, LANES), F32),
                        pltpu.VMEM((2, n_chunks, 2 * CHUNK, LANES), BF16),
                        pltpu.VMEM((2, n_chunks, CHUNK, LANES), F32),
                        pltpu.VMEM((2, n_chunks, LANES, LANES), BF16),
                        pltpu.VMEM((2, n_chunks, LANES, LANES), F32),
                        pltpu.VMEM((2, n_chunks, 8, LANES), F32)],
        compiler_params=pltpu.CompilerParams(
            dimension_semantics=("arbitrary", "arbitrary", "arbitrary"),
            vmem_limit_bytes=VMEM_LIMIT),
        name="rwkv_scan",
    )(r, lw, k, v, kk, a)


SB_LOG_ZERO = -104.0
SB_ROW_PARTS = 2


def _sb_kernel(blk, q_ref, k_ref, v_ref, o_ref, acc_ref, carry_ref):
    lane = lax.broadcasted_iota(jnp.int32, (1, LANES), 1)
    in_h0 = lane < HEAD_DIM
    row = lax.broadcasted_iota(jnp.int32, (blk, blk), 0)
    col = lax.broadcasted_iota(jnp.int32, (blk, blk), 1)
    causal = col < row
    tri = jnp.where(row > col, 1.0, 0.0).astype(BF16)

    tri_sum = jnp.concatenate([tri, jnp.ones((blk, LANES), BF16)], axis=1)
    n_parts = SB_ROW_PARTS
    part = blk // n_parts
    chains = [(h, p) for h in (0, 1) for p in range(n_parts)]
    rows_of = lambda p: slice(p * part, (p + 1) * part)
    reps = blk // LANES

    def one_block(q_heads, j, masked):
        start = pl.multiple_of(j * blk, blk)
        kj = k_ref[0, pl.ds(start, blk), :]
        vj = v_ref[0, pl.ds(start, blk), :]
        z = [_dot_nt(q_heads[h][rows_of(p)], kj) for h, p in chains]
        sp = [jnp.maximum(x, 0.0) + jnp.log(1.0 + jnp.exp(-jnp.abs(x))) for x in z]
        lk = [-x for x in sp]
        if masked:
            lk = [jnp.where(causal[rows_of(p)], x, 0.0) for (h, p), x in zip(chains, lk)]
        sums = [_dot(x.astype(BF16), tri_sum) for x in lk]
        carry = [carry_ref[h, rows_of(p), :] for h, p in chains]
        att = [jnp.exp(z[c] - sp[c] + sums[c][:, :blk] + jnp.concatenate([carry[c]] * reps, axis=1))
               for c in range(len(chains))]
        if masked:
            att = [jnp.where(causal[rows_of(p)], x, 0.0) for (h, p), x in zip(chains, att)]
        new_carry = [carry[c] + sums[c][:, blk:] for c in range(len(chains))]
        for c, (h, p) in enumerate(chains):
            acc_ref[h, rows_of(p), :] += _dot(att[c].astype(BF16), vj)
            carry_ref[h, rows_of(p), :] = new_carry[c]
        top = new_carry[0]
        for x in new_carry[1:]:
            top = jnp.maximum(top, x)
        return jnp.max(top)

    def cond(state):
        j, top = state
        return jnp.logical_and(j >= 0, top > SB_LOG_ZERO)

    def query_block(qi, unused):
        q_start = pl.multiple_of(qi * blk, blk)
        q = q_ref[0, pl.ds(q_start, blk), :]
        zero = jnp.zeros_like(q)
        q_heads = (jnp.where(in_h0, q, zero), jnp.where(in_h0, zero, q))
        acc_ref[...] = jnp.zeros_like(acc_ref)
        carry_ref[...] = jnp.zeros_like(carry_ref)
        top = one_block(q_heads, qi, True)

        def body(state):
            j, _ = state
            return j - 1, one_block(q_heads, j, False)

        lax.while_loop(cond, body, (qi - 1, top))
        o_ref[0, pl.ds(q_start, blk), :] = jnp.where(in_h0, acc_ref[0], acc_ref[1])
        return unused

    lax.fori_loop(0, q_ref.shape[1] // blk, query_block, 0)


def _sb_attention(q, k, v, blk):
    B, T, W = q.shape
    n_pairs = W // LANES
    seq_spec = pl.BlockSpec((1, T, LANES), lambda b, h: (b, 0, h))
    return pl.pallas_call(
        functools.partial(_sb_kernel, blk),
        grid=(B, n_pairs),
        in_specs=[seq_spec, seq_spec, seq_spec],
        out_specs=seq_spec,
        out_shape=jax.ShapeDtypeStruct((B, T, W), F32),
        scratch_shapes=[pltpu.VMEM((2, blk, LANES), F32), pltpu.VMEM((2, blk, LANES), F32)],
        compiler_params=pltpu.CompilerParams(
            dimension_semantics=("arbitrary", "arbitrary"),
            vmem_limit_bytes=VMEM_LIMIT),
        name="sb_attention",
    )(q, k, v)


def _rms(x, g):
    return x * lax.rsqrt(jnp.mean(x * x, axis=-1, keepdims=True) + NORM_EPS) * g


def _split2(x):
    hi = x.astype(BF16)
    return hi, (x - hi.astype(F32)).astype(BF16)


def _mix_out_kernel(has_router, y_ref, bonus_ref, gate_ref, o_ref, x_ref,
                    lnx_g_ref, lnx_b_ref, sb_g_ref, w_out_ref, post_ref, pre_ref, router_ref,
                    h_o, hn_o, route_o):
    ones_bd = _head_ones(HEAD_SUM_BLOCK)
    inv_n = 1.0 / HEAD_DIM
    y = y_ref[0]
    mean = _head_sum(y, ones_bd) * inv_n
    yc = y - mean
    var = _head_sum(yc * yc, ones_bd) * inv_n
    ya = (yc * lax.rsqrt(var + LNX_EPS) * lnx_g_ref[...] + lnx_b_ref[...] + bonus_ref[0]) * gate_ref[0]
    o = o_ref[0]
    yb = o * lax.rsqrt(_head_sum(o * o, ones_bd) * inv_n + NORM_EPS) * sb_g_ref[...]
    cat = jnp.concatenate([ya, yb], axis=-1).astype(BF16)
    y2 = _dot(cat, w_out_ref[...])
    h = x_ref[0] + _rms(y2, post_ref[...])
    hn = _rms(h, pre_ref[...])
    h_o[0] = h
    if has_router:
        hn_o[...] = _slab(hn)
        hi, lo = _split2(hn)
        r_hi, r_lo = _split2(router_ref[...])
        logits = _dot(hi, r_hi) + (_dot(lo, r_hi) + _dot(hi, r_lo))
        lane = lax.broadcasted_iota(jnp.int32, logits.shape, 1)
        neg = jnp.float32(-jnp.inf)
        logits = jnp.where(lane < N_EXPERTS, logits, neg)
        m1 = jnp.max(logits, axis=-1, keepdims=True)
        i1 = jnp.min(jnp.where(logits == m1, lane, LANES), axis=-1, keepdims=True)
        rest = jnp.where(lane == i1, neg, logits)
        m2 = jnp.max(rest, axis=-1, keepdims=True)
        i2 = jnp.min(jnp.where(rest == m2, lane, LANES), axis=-1, keepdims=True)
        e2 = jnp.exp(m2 - m1)
        g1 = 1.0 / (1.0 + e2)
        route = jnp.where(lane == 0, g1, jnp.where(lane == 1, e2 * g1, 0.0))
        route = jnp.where(lane == 2, i1.astype(F32), jnp.where(lane == 3, i2.astype(F32), route))
        route_o[0] = route
    else:
        hn_o[...] = hn.astype(BF16)
        route_o[0] = jnp.zeros_like(route_o[0])


def _mix_out(y, bonus, gate, o, x, p, tile):
    B, T, D = x.shape
    has_router = 'router' in p
    row = lambda a: a.reshape(1, -1).astype(F32)
    router = _pad_cols(p['router'], LANES) if has_router else jnp.zeros((D, LANES), F32)
    full = lambda shape: pl.BlockSpec(shape, lambda b, t: (0,) * len(shape))
    tok = lambda width: pl.BlockSpec((1, tile, width), lambda b, t: (b, t, 0))
    n_t = T // tile
    if has_router:
        assert D == ROW_SLABS * LANES
        hn_spec = pl.BlockSpec((tile * ROW_SLABS, LANES), lambda b, t: (b * n_t + t, 0))
        hn_shape = jax.ShapeDtypeStruct((B * T * ROW_SLABS, LANES), F32)
    else:
        hn_spec = pl.BlockSpec((tile, D), lambda b, t: (b * n_t + t, 0))
        hn_shape = jax.ShapeDtypeStruct((B * T, D), BF16)
    return pl.pallas_call(
        functools.partial(_mix_out_kernel, has_router),
        grid=(B, n_t),
        in_specs=[tok(RWKV_W)] * 4 + [tok(D)] +
                 [full((1, RWKV_W))] * 3 + [full((D, D)), full((1, D)), full((1, D)), full((D, LANES))],
        out_specs=[tok(D), hn_spec, tok(LANES)],
        out_shape=[jax.ShapeDtypeStruct((B, T, D), F32), hn_shape,
                   jax.ShapeDtypeStruct((B, T, LANES), F32)],
        compiler_params=pltpu.CompilerParams(
            dimension_semantics=("arbitrary", "arbitrary"), vmem_limit_bytes=VMEM_LIMIT),
        name="mix_out",
    )(y, bonus, gate, o, x, row(p['lnx_g']), row(p['lnx_b']), row(p['sb_g']),
      p['w_out'].astype(BF16), row(p['mix_post']), row(p['ffn_pre']), router)


def _ffn_kernel(h_ref, hn_ref, wg_ref, wu_ref, wd_ref, post_ref, o_ref, acc_ref):
    f = pl.program_id(1)

    @pl.when(f == 0)
    def _():
        acc_ref[...] = jnp.zeros_like(acc_ref)

    hn = hn_ref[...]
    g = _dot(hn, wg_ref[...])
    u = _dot(hn, wu_ref[...])
    act = (g * jax.nn.sigmoid(g) * u).astype(BF16)
    acc_ref[...] += _dot(act, wd_ref[...])

    @pl.when(f == pl.num_programs(1) - 1)
    def _():
        o_ref[...] = h_ref[...] + _rms(acc_ref[...], post_ref[...])


def _ffn(h, hn, w_gate, w_up, w_down, post, tile_m, tile_f):
    N, D = h.shape
    F = w_gate.shape[1]
    return pl.pallas_call(
        _ffn_kernel,
        grid=(N // tile_m, F // tile_f),
        in_specs=[pl.BlockSpec((tile_m, D), lambda i, f: (i, 0)),
                  pl.BlockSpec((tile_m, D), lambda i, f: (i, 0)),
                  pl.BlockSpec((D, tile_f), lambda i, f: (0, f)),
                  pl.BlockSpec((D, tile_f), lambda i, f: (0, f)),
                  pl.BlockSpec((tile_f, D), lambda i, f: (f, 0)),
                  pl.BlockSpec((1, D), lambda i, f: (0, 0))],
        out_specs=pl.BlockSpec((tile_m, D), lambda i, f: (i, 0)),
        out_shape=jax.ShapeDtypeStruct((N, D), F32),
        scratch_shapes=[pltpu.VMEM((tile_m, D), F32)],
        compiler_params=pltpu.CompilerParams(
            dimension_semantics=("arbitrary", "arbitrary"), vmem_limit_bytes=VMEM_LIMIT),
        name="ffn",
    )(h, hn, w_gate.astype(BF16), w_up.astype(BF16), w_down.astype(BF16),
      post.reshape(1, -1).astype(F32))


GATHER_UNROLL = 8


def _gather_rows_kernel(idx_ref, src_ref, out_ref, sem):
    n = out_ref.shape[0]

    def issue(g, carry):
        for u in range(GATHER_UNROLL):
            i = g * GATHER_UNROLL + u
            pltpu.make_async_copy(src_ref.at[idx_ref[0, 0, i]], out_ref.at[i], sem).start(
                priority=u % 2)
        return carry

    lax.fori_loop(0, n // GATHER_UNROLL, issue, 0)
    pltpu.make_async_copy(src_ref.at[pl.ds(0, n)], out_ref, sem).wait()


def _gather_rows(src, idx, tile):
    n_src = src.shape[0] // ROW_SLABS
    n_out = idx.shape[0]
    n_tiles = n_out // tile
    out = pl.pallas_call(
        _gather_rows_kernel,
        grid=(n_tiles,),
        in_specs=[pl.BlockSpec((1, 1, tile), lambda i: (i, 0, 0), memory_space=pltpu.SMEM),
                  pl.BlockSpec(memory_space=pl.ANY)],
        out_specs=pl.BlockSpec((tile, ROW_SLABS, LANES), lambda i: (i, 0, 0)),
        out_shape=jax.ShapeDtypeStruct((n_out, ROW_SLABS, LANES), F32),
        scratch_shapes=[pltpu.SemaphoreType.DMA(())],
        compiler_params=pltpu.CompilerParams(
            dimension_semantics=("arbitrary",), vmem_limit_bytes=VMEM_LIMIT),
        name="gather_rows",
    )(idx.reshape(n_tiles, 1, tile), src.reshape(n_src, ROW_SLABS, LANES))
    return out.reshape(n_out * ROW_SLABS, LANES)


def _scatter_rows_kernel(idx_ref, src_ref, out_ref, sem):
    n = src_ref.shape[0]

    def issue(g, carry):
        for u in range(GATHER_UNROLL):
            i = g * GATHER_UNROLL + u
            pltpu.make_async_copy(src_ref.at[i], out_ref.at[idx_ref[0, 0, i]], sem).start(
                priority=u % 2)
        return carry

    lax.fori_loop(0, n // GATHER_UNROLL, issue, 0)
    pltpu.make_async_copy(src_ref, out_ref.at[pl.ds(0, n)], sem).wait()


def _scatter_rows(src, idx, tile):
    n_src = src.shape[0] // ROW_SLABS
    n_out = idx.shape[0]
    n_tiles = n_out // tile
    src_tiles = n_src // tile
    out = pl.pallas_call(
        _scatter_rows_kernel,
        grid=(n_tiles,),
        in_specs=[pl.BlockSpec((1, 1, tile), lambda i: (i, 0, 0), memory_space=pltpu.SMEM),
                  pl.BlockSpec((tile, ROW_SLABS, LANES), lambda i: (lax.rem(i, src_tiles), 0, 0))],
        out_specs=pl.BlockSpec(memory_space=pl.ANY),
        out_shape=jax.ShapeDtypeStruct((n_out, ROW_SLABS, LANES), F32),
        scratch_shapes=[pltpu.SemaphoreType.DMA(())],
        compiler_params=pltpu.CompilerParams(
            dimension_semantics=("arbitrary",), vmem_limit_bytes=VMEM_LIMIT),
        name="scatter_rows",
    )(idx.reshape(n_tiles, 1, tile), src.reshape(n_src, ROW_SLABS, LANES))
    return out.reshape(n_out * ROW_SLABS, LANES)


def _unslab(ref, rows):
    return ref[...].reshape(rows, ROW_SLABS, LANES).reshape(rows, ROW_SLABS * LANES)


def _slab(x):
    rows = x.shape[0]
    return x.reshape(rows, ROW_SLABS, LANES).reshape(rows * ROW_SLABS, LANES)


def _moe_kernel(te_ref, nv_ref, x_ref, wg_ref, wu_ref, wd_ref, y_ref, acc_ref, xb_ref):
    i = pl.program_id(0)
    f = pl.program_id(1)
    rows = xb_ref.shape[0]
    valid = i < nv_ref[0]

    @pl.when(jnp.logical_and(valid, f == 0))
    def _():
        acc_ref[...] = jnp.zeros_like(acc_ref)
        xb_ref[...] = _unslab(x_ref, rows).astype(BF16)

    @pl.when(valid)
    def _():
        xb = xb_ref[...]
        g = _dot(xb, wg_ref[0])
        u = _dot(xb, wu_ref[0])
        act = (g * jax.nn.sigmoid(g) * u).astype(BF16)
        acc_ref[...] += _dot(act, wd_ref[0])

    @pl.when(f == pl.num_programs(1) - 1)
    def _():
        y_ref[...] = _slab(jnp.where(valid, acc_ref[...], 0.0))


def _moe_experts(x_sorted, tile_expert, n_valid, w_gate, w_up, w_down, tile_m, tile_f):
    E, D, F = w_gate.shape
    R = x_sorted.shape[0] // ROW_SLABS
    n_f = F // tile_f

    def w_col(i, f, te, nv):
        return (te[i], 0, jnp.where(i < nv[0], f, n_f - 1))

    def w_row(i, f, te, nv):
        return (te[i], jnp.where(i < nv[0], f, n_f - 1), 0)

    slab = pl.BlockSpec((tile_m * ROW_SLABS, LANES), lambda i, f, te, nv: (i, 0))
    return pl.pallas_call(
        _moe_kernel,
        grid_spec=pltpu.PrefetchScalarGridSpec(
            num_scalar_prefetch=2,
            grid=(R // tile_m, n_f),
            in_specs=[slab,
                      pl.BlockSpec((1, D, tile_f), w_col),
                      pl.BlockSpec((1, D, tile_f), w_col),
                      pl.BlockSpec((1, tile_f, D), w_row)],
            out_specs=slab,
            scratch_shapes=[pltpu.VMEM((tile_m, D), F32), pltpu.VMEM((tile_m, D), BF16)]),
        out_shape=jax.ShapeDtypeStruct((R * ROW_SLABS, LANES), F32),
        compiler_params=pltpu.CompilerParams(
            dimension_semantics=("arbitrary", "arbitrary"), vmem_limit_bytes=VMEM_LIMIT),
        name="moe_experts",
    )(tile_expert, n_valid, x_sorted, w_gate.astype(BF16), w_up.astype(BF16), w_down.astype(BF16))


def _moe_combine_kernel(h_ref, y1_ref, y2_ref, route_ref, post_ref, o_ref):
    rows = h_ref.shape[0]
    route = route_ref[...]
    f = route[:, 0:1] * _unslab(y1_ref, rows) + route[:, 1:2] * _unslab(y2_ref, rows)
    o_ref[...] = h_ref[...] + _rms(f, post_ref[...])


def _moe_combine(h, ys, route, post, tile):
    N, D = h.shape
    n_t = N // tile
    return pl.pallas_call(
        _moe_combine_kernel,
        grid=(n_t,),
        in_specs=[pl.BlockSpec((tile, D), lambda i: (i, 0)),
                  pl.BlockSpec((tile * ROW_SLABS, LANES), lambda i: (i, 0)),
                  pl.BlockSpec((tile * ROW_SLABS, LANES), lambda i: (n_t + i, 0)),
                  pl.BlockSpec((tile, LANES), lambda i: (i, 0)),
                  pl.BlockSpec((1, D), lambda i: (0, 0))],
        out_specs=pl.BlockSpec((tile, D), lambda i: (i, 0)),
        out_shape=jax.ShapeDtypeStruct((N, D), F32),
        compiler_params=pltpu.CompilerParams(
            dimension_semantics=("arbitrary",), vmem_limit_bytes=VMEM_LIMIT),
        name="moe_combine",
    )(h, ys, ys, route, post.reshape(1, -1).astype(F32))


def _moe(h, hn_slabs, route, w_gate, w_up, w_down, post, tile_m, tile_f):
    N, D = h.shape
    E = w_gate.shape[0]
    n_slots = 2 * N
    R = n_slots + E * tile_m
    n_tiles = R // tile_m

    experts = jnp.concatenate([route[:, 2], route[:, 3]]).astype(jnp.int32)
    onehot = (experts[:, None] == jnp.arange(E, dtype=jnp.int32)[None, :]).astype(jnp.int32)
    ranks = jnp.cumsum(onehot, axis=0) - onehot
    counts = jnp.sum(onehot, axis=0)
    padded = ((counts + tile_m - 1) // tile_m) * tile_m
    ends = jnp.cumsum(padded)
    pos = jnp.sum(onehot * (ranks + (ends - padded)[None, :]), axis=1)
    gap_begin = jnp.concatenate([ends - padded + counts, ends[E - 1:]])
    gap_size = jnp.concatenate([padded - counts, R - ends[E - 1:]])
    gap_end_idx = jnp.cumsum(gap_size)
    j = jnp.arange(R - n_slots, dtype=jnp.int32)
    seg = jnp.sum((j[:, None] >= gap_end_idx[None, :]).astype(jnp.int32), axis=1)
    seg_hot = (seg[:, None] == jnp.arange(E + 1, dtype=jnp.int32)[None, :]).astype(jnp.int32)
    fill_rows = j + jnp.sum(seg_hot * (gap_begin - (gap_end_idx - gap_size))[None, :], axis=1)
    row_of_slot = jnp.concatenate([pos, fill_rows]).astype(jnp.int32)
    tile_start = jnp.arange(n_tiles, dtype=jnp.int32) * tile_m
    tile_expert = jnp.minimum(
        jnp.sum((ends[None, :] <= tile_start[:, None]).astype(jnp.int32), axis=1), E - 1)
    n_valid = (ends[E - 1] // tile_m).reshape(1).astype(jnp.int32)

    x_sorted = _scatter_rows(hn_slabs, row_of_slot, tile_m)
    y_sorted = _moe_experts(x_sorted, tile_expert, n_valid, w_gate, w_up, w_down, tile_m, tile_f)
    ys = _gather_rows(y_sorted, pos, tile_m)
    return _moe_combine(h, ys, route, post, tile_m)


MIX_TILE = 512
RWKV_TILE = 512
SB_BLOCK = 256
FFN_TILE_M = 1024
FFN_TILE_F = 896
MOE_TILE_M = 512
MOE_TILE_F = 896


def _layer(x, p, v_first):
    B, T, D = x.shape
    r, lw, k, v, kk, a, gate, bonus, q, sk, sv = _mix_in(x, p, v_first, MIX_TILE)
    y = _rwkv_scan(r, lw, k, v, kk, a, RWKV_TILE)
    o = _sb_attention(q, sk, sv, SB_BLOCK)
    h, hn, route = _mix_out(y, bonus, gate, o, x, p, MIX_TILE)
    h2 = h.reshape(B * T, D)
    if 'router' in p:
        out = _moe(h2, hn, route.reshape(B * T, LANES), p['moe_gate'], p['moe_up'], p['moe_down'],
                   p['ffn_post'], MOE_TILE_M, MOE_TILE_F)
    else:
        out = _ffn(h2, hn, p['ff_gate'], p['ff_up'], p['ff_down'], p['ffn_post'],
                   FFN_TILE_M, FFN_TILE_F)
    return out.reshape(B, T, D), (v if v_first is None else v_first)


def kernel(x, mix_pre_0, w_in_0, mu_0, decay0_0, decay_up_0, iclr0_0, iclr_up_0, gate_up_0, k_k_0, k_a_0, r_k_0, lnx_g_0, lnx_b_0, sb_g_0, w_out_0, mix_post_0, ffn_pre_0, ff_gate_0, ff_up_0, ff_down_0, ffn_post_0, mix_pre_1, w_in_1, mu_1, decay0_1, decay_up_1, iclr0_1, iclr_up_1, gate_up_1, vres0_1, vres_up_1, k_k_1, k_a_1, r_k_1, lnx_g_1, lnx_b_1, sb_g_1, w_out_1, mix_post_1, ffn_pre_1, router_1, moe_gate_1, moe_up_1, moe_down_1, ffn_post_1):
    p0 = dict(mix_pre=mix_pre_0, w_in=w_in_0, mu=mu_0, decay0=decay0_0, decay_up=decay_up_0,
              iclr0=iclr0_0, iclr_up=iclr_up_0, gate_up=gate_up_0, k_k=k_k_0, k_a=k_a_0,
              r_k=r_k_0, lnx_g=lnx_g_0, lnx_b=lnx_b_0, sb_g=sb_g_0, w_out=w_out_0,
              mix_post=mix_post_0, ffn_pre=ffn_pre_0, ff_gate=ff_gate_0, ff_up=ff_up_0,
              ff_down=ff_down_0, ffn_post=ffn_post_0)
    p1 = dict(mix_pre=mix_pre_1, w_in=w_in_1, mu=mu_1, decay0=decay0_1, decay_up=decay_up_1,
              iclr0=iclr0_1, iclr_up=iclr_up_1, gate_up=gate_up_1, vres0=vres0_1,
              vres_up=vres_up_1, k_k=k_k_1, k_a=k_a_1, r_k=r_k_1, lnx_g=lnx_g_1, lnx_b=lnx_b_1,
              sb_g=sb_g_1, w_out=w_out_1, mix_post=mix_post_1, ffn_pre=ffn_pre_1,
              router=router_1, moe_gate=moe_gate_1, moe_up=moe_up_1, moe_down=moe_down_1,
              ffn_post=ffn_post_1)
    x, v_first = _layer(x, p0, None)
    x, _ = _layer(x, p1, v_first)
    return x
```

```python
import functools
import math

import jax
import jax.numpy as jnp
from jax import lax
from jax.experimental import pallas as pl
from jax.experimental.pallas import tpu as pltpu

HEAD_DIM = 64
RWKV_W = 512
SB_W = 512
N_EXPERTS = 8
NORM_EPS = 1e-6
LNX_EPS = 64e-5
DECAY_SCALE = math.exp(-0.5)
LANES = 128
CHUNK = 64
LOW_SLOT = 128
HEAD_SUM_BLOCK = 128
ROW_SLABS = 8
VMEM_LIMIT = 56 * 1024 * 1024

F32 = jnp.float32
BF16 = jnp.bfloat16


def _dot(a, b):
    return jnp.dot(a, b, preferred_element_type=F32)


def _dot_nt(a, b):
    return lax.dot_general(a, b, (((1,), (1,)), ((), ())), preferred_element_type=F32)


def _dot_tn(a, b):
    return lax.dot_general(a, b, (((0,), (0,)), ((), ())), preferred_element_type=F32)


def _head_ones(width):
    r = lax.broadcasted_iota(jnp.int32, (width, width), 0) // HEAD_DIM
    c = lax.broadcasted_iota(jnp.int32, (width, width), 1) // HEAD_DIM
    return jnp.where(r == c, 1.0, 0.0).astype(BF16)


def _head_sum(x, ones_bd):
    hi = x.astype(BF16)
    lo = (x - hi.astype(F32)).astype(BF16)
    w = ones_bd.shape[0]
    blocks = [_dot(hi[:, c:c + w], ones_bd) + _dot(lo[:, c:c + w], ones_bd)
              for c in range(0, x.shape[1], w)]
    return jnp.concatenate(blocks, axis=1)


def _mix_in_kernel(has_vres, x_ref, g_ref, w_ref, mu_ref, decay0_ref, decay_up_ref,
                   iclr0_ref, iclr_up_ref, gate_up_ref, vres0_ref, vres_up_ref,
                   kk_g_ref, ka_ref, rk_ref, vfirst_ref,
                   r_o, lw_o, k_o, v_o, kk_o, a_o, gate_o, bonus_o, q_o, sk_o, sv_o,
                   carry_ref):
    t = pl.program_id(1)
    n_shift = 3 * RWKV_W + 4 * LOW_SLOT

    @pl.when(t == 0)
    def _():
        carry_ref[...] = jnp.zeros_like(carry_ref)

    x = x_ref[0]
    hn = x * lax.rsqrt(jnp.mean(x * x, axis=-1, keepdims=True) + NORM_EPS) * g_ref[...]
    p = _dot(hn.astype(BF16), w_ref[...])
    pr = p[:, :n_shift]
    rows = pr.shape[0]
    rolled = pltpu.roll(pr, 1, 0)
    first = lax.broadcasted_iota(jnp.int32, pr.shape, 0) == 0
    prev = jnp.where(first, carry_ref[0:1, :], rolled)
    carry_ref[0:1, :] = pr[rows - 1:rows, :]
    s = pr + mu_ref[...] * (prev - pr)

    r = s[:, 0:RWKV_W]
    k = s[:, RWKV_W:2 * RWKV_W]
    v = s[:, 2 * RWKV_W:3 * RWKV_W]
    base = 3 * RWKV_W
    w_dn = s[:, base:base + LOW_SLOT]
    a_dn = s[:, base + LOW_SLOT:base + 2 * LOW_SLOT]
    g_dn = s[:, base + 2 * LOW_SLOT:base + 3 * LOW_SLOT]
    logw = -DECAY_SCALE * jax.nn.sigmoid(
        decay0_ref[...] + _dot(jnp.tanh(w_dn).astype(BF16), decay_up_ref[...]))
    iclr = jax.nn.sigmoid(iclr0_ref[...] + _dot(a_dn.astype(BF16), iclr_up_ref[...]))
    gate = _dot(jax.nn.sigmoid(g_dn).astype(BF16), gate_up_ref[...])
    if has_vres:
        vr_dn = s[:, base + 3 * LOW_SLOT:base + 4 * LOW_SLOT]
        mix = jax.nn.sigmoid(vres0_ref[...] + _dot(vr_dn.astype(BF16), vres_up_ref[...]))
        v = v + (vfirst_ref[0] - v) * mix

    ones_bd = _head_ones(HEAD_SUM_BLOCK)
    kk = k * kk_g_ref[...]
    kk = kk * lax.rsqrt(jnp.maximum(_head_sum(kk * kk, ones_bd), 1e-12))
    k = k * (1.0 + (iclr - 1.0) * ka_ref[...])
    bonus = _head_sum(r * k * rk_ref[...], ones_bd) * v

    r_o[0] = r
    lw_o[0] = logw
    k_o[0] = k
    v_o[0] = v
    kk_o[0] = kk
    a_o[0] = iclr
    gate_o[0] = gate
    bonus_o[0] = bonus
    q_o[0] = (p[:, n_shift:n_shift + SB_W] * (HEAD_DIM ** -0.5)).astype(BF16)
    sk_o[0] = p[:, n_shift + SB_W:n_shift + 2 * SB_W].astype(BF16)
    sv_o[0] = p[:, n_shift + 2 * SB_W:n_shift + 3 * SB_W].astype(BF16)


def _pad_cols(a, width):
    return jnp.pad(a, ((0, 0), (0, width - a.shape[1])))


def _pad_rows(a, height):
    return jnp.pad(a, ((0, height - a.shape[0]), (0, 0)))


def _mix_in(x, p, v_first, tile):
    B, T, D = x.shape
    has_vres = v_first is not None
    w_in, mu = p['w_in'], p['mu']
    ranks = [32, 32, 96] + ([32] if has_vres else [])
    off = 3 * RWKV_W
    low_w, low_mu = [], []
    for rk in ranks:
        low_w.append(_pad_cols(w_in[:, off:off + rk], LOW_SLOT))
        low_mu.append(jnp.pad(mu[off:off + rk], (0, LOW_SLOT - rk)))
        off += rk
    if not has_vres:
        low_w.append(jnp.zeros((D, LOW_SLOT), F32))
        low_mu.append(jnp.zeros((LOW_SLOT,), F32))
    w_all = jnp.concatenate([w_in[:, :3 * RWKV_W]] + low_w + [w_in[:, off:]], axis=1).astype(BF16)
    mu_all = jnp.concatenate([mu[:3 * RWKV_W]] + low_mu)[None, :]
    n_shift = 3 * RWKV_W + 4 * LOW_SLOT
    n_cols = n_shift + 3 * SB_W
    assert w_all.shape == (D, n_cols)

    row = lambda a: a.reshape(1, -1).astype(F32)
    up = lambda a: _pad_rows(a, LOW_SLOT).astype(BF16)
    if has_vres:
        vres0, vres_up, vf = row(p['vres0']), up(p['vres_up']), v_first
    else:
        vres0 = jnp.zeros((1, RWKV_W), F32)
        vres_up = jnp.zeros((LOW_SLOT, RWKV_W), BF16)
        vf = jnp.zeros((B, 8, RWKV_W), F32)

    grid = (B, T // tile)
    full = lambda shape: pl.BlockSpec(shape, lambda b, t: (0,) * len(shape))
    tok = lambda width: pl.BlockSpec((1, tile, width), lambda b, t: (b, t, 0))
    vf_spec = tok(RWKV_W) if has_vres else pl.BlockSpec((1, 8, RWKV_W), lambda b, t: (b, 0, 0))
    f32_out = jax.ShapeDtypeStruct((B, T, RWKV_W), F32)
    bf_out = jax.ShapeDtypeStruct((B, T, SB_W), BF16)
    outs = pl.pallas_call(
        functools.partial(_mix_in_kernel, has_vres),
        grid=grid,
        in_specs=[tok(D), full((1, D)), full((D, n_cols)), full((1, n_shift)),
                  full((1, RWKV_W)), full((LOW_SLOT, RWKV_W)),
                  full((1, RWKV_W)), full((LOW_SLOT, RWKV_W)),
                  full((LOW_SLOT, RWKV_W)),
                  full((1, RWKV_W)), full((LOW_SLOT, RWKV_W)),
                  full((1, RWKV_W)), full((1, RWKV_W)), full((1, RWKV_W)),
                  vf_spec],
        out_specs=[tok(RWKV_W)] * 8 + [tok(SB_W)] * 3,
        out_shape=[f32_out] * 8 + [bf_out] * 3,
        scratch_shapes=[pltpu.VMEM((8, n_shift), F32)],
        compiler_params=pltpu.CompilerParams(
            dimension_semantics=("arbitrary", "arbitrary"), vmem_limit_bytes=VMEM_LIMIT),
        name="mix_in",
    )(x, row(p['mix_pre']), w_all, mu_all,
      row(p['decay0']), up(p['decay_up']), row(p['iclr0']), up(p['iclr_up']), up(p['gate_up']),
      vres0, vres_up, row(p['k_k']), row(p['k_a']), row(p['r_k']), vf)
    return outs


def _stack(q, m0, m1):
    return jnp.concatenate([q * m0, q * m1], axis=0)


def _rwkv_kernel(n_chunks, r_ref, lw_ref, k_ref, v_ref, kk_ref, a_ref, y_ref,
                 s_ref, rhat_sc, yin_sc, amat_sc, dmat_sc, decay_sc):
    t = pl.program_id(2)
    cur = lax.rem(t, 2)
    prev = 1 - cur
    C = CHUNK

    @pl.when(t == 0)
    def _():
        s_ref[...] = jnp.zeros_like(s_ref)
        rhat_sc[1] = jnp.zeros_like(rhat_sc[1])
        yin_sc[1] = jnp.zeros_like(yin_sc[1])
        amat_sc[1] = jnp.zeros_like(amat_sc[1])
        dmat_sc[1] = jnp.zeros_like(dmat_sc[1])
        decay_sc[1] = jnp.zeros_like(decay_sc[1])

    scan = {'s': s_ref[...], 'done': 0}
    n_slots = 8

    def scan_steps(slot):
        upto = ((slot + 1) * n_chunks) // n_slots
        for c in range(scan['done'], upto):
            s = scan['s']
            s_b = s.astype(BF16)
            y_st = _dot_nt(rhat_sc[prev, c], s_b)
            y_ref[0, c * C:(c + 1) * C, :] = y_st[:C, :] + y_st[C:, :] + yin_sc[prev, c]
            scan['s'] = (s * decay_sc[prev, c][0:1, :] + _dot(s_b, amat_sc[prev, c])
                         + dmat_sc[prev, c])
        scan['done'] = upto

    C2 = 2 * C
    lane = lax.broadcasted_iota(jnp.int32, (1, LANES), 1)
    m0 = jnp.where(lane < HEAD_DIM, 1.0, 0.0)
    m1 = 1.0 - m0
    ti = lax.broadcasted_iota(jnp.int32, (C, C), 0)
    tj = lax.broadcasted_iota(jnp.int32, (C, C), 1)
    tri = jnp.where(tj <= ti, 1.0, 0.0).astype(BF16)
    r2 = lax.broadcasted_iota(jnp.int32, (C2, C2), 0)
    c2 = lax.broadcasted_iota(jnp.int32, (C2, C2), 1)
    same_head = (r2 // C) == (c2 // C)
    strict = jnp.logical_and(same_head, c2 < r2)
    incl = jnp.logical_and(same_head, c2 <= r2)
    eye = jnp.where(r2 == c2, 1.0, 0.0)

    def bf(a):
        return a.astype(BF16)

    r = r_ref[0]
    lw = lw_ref[0]
    k = k_ref[0]
    v = v_ref[0]
    kk = kk_ref[0]
    b = kk * a_ref[0]

    lw_hi = bf(lw)
    lw_mid = bf(lw - lw_hi.astype(F32))
    lw_lo = bf(lw - lw_hi.astype(F32) - lw_mid.astype(F32))
    cums = []
    for c in range(n_chunks):
        sl = slice(c * C, (c + 1) * C)
        cums.append(_dot(tri, lw_hi[sl]) + (_dot(tri, lw_mid[sl]) + _dot(tri, lw_lo[sl])))
    cum = jnp.concatenate(cums, axis=0)
    total = jnp.concatenate(
        [jnp.broadcast_to(cc[C - 1:C, :], (C, LANES)) for cc in cums], axis=0)
    scan_steps(0)
    e_neg = jnp.exp(-cum)
    e_tail = jnp.exp(total - cum)
    rt_all = r * jnp.exp(cum)
    zt_all = -kk * jnp.exp(cum - lw)
    bt_all = b * e_neg
    kt_all = k * e_neg
    bh_all = b * e_tail
    kh_all = k * e_tail
    decay_all = jnp.exp(total)

    chunks = range(n_chunks)
    cut = lambda x, c: _stack(x[c * C:(c + 1) * C, :], m0, m1)
    rt = [cut(rt_all, c) for c in chunks]
    zt = [cut(zt_all, c) for c in chunks]
    bh = [bf(cut(bh_all, c)) for c in chunks]
    kh = [bf(cut(kh_all, c)) for c in chunks]
    vs = [bf(cut(v, c)) for c in chunks]
    g = [_dot_nt(bf(jnp.concatenate([zt[c], rt[c]], axis=0)),
                 bf(jnp.concatenate([cut(bt_all, c), cut(kt_all, c)], axis=0))) for c in chunks]
    scan_steps(1)
    l_zb = [jnp.where(strict, g[c][:C2, :C2], 0.0) for c in chunks]
    l_zk = [bf(jnp.where(strict, g[c][:C2, C2:], 0.0)) for c in chunks]
    l_rb = [bf(jnp.where(incl, g[c][C2:, :C2], 0.0)) for c in chunks]
    l_rk = [bf(jnp.where(incl, g[c][C2:, C2:], 0.0)) for c in chunks]

    inv = [eye + l_zb[c] for c in chunks]
    pw = l_zb
    for i in range(int(math.log2(C)) - 1):
        pw_b = [bf(p) for p in pw]
        pw = [_dot(p, p) for p in pw_b]
        inv = [inv[c] + _dot(bf(inv[c]), bf(pw[c])) for c in chunks]
        scan_steps(2 + i)

    zk_v = [_dot(l_zk[c], vs[c]) for c in chunks]
    zp = [_dot(bf(inv[c]), bf(jnp.concatenate([zt[c], zk_v[c]], axis=1))) for c in chunks]
    scan_steps(n_slots - 1)
    s_ref[...] = scan['s']
    zp_b = [bf(x) for x in zp]
    rk_v = [_dot(l_rk[c], vs[c]) for c in chunks]
    ry = [_dot(l_rb[c], zp_b[c]) for c in chunks]
    r_hat = [bf(rt[c] + ry[c][:, :LANES]) for c in chunks]
    y_in = [rk_v[c] + ry[c][:, LANES:] for c in chunks]
    a_mat = [bf(_dot_tn(zp_b[c][:, :LANES], bh[c])) for c in chunks]
    d_mat = [_dot_tn(jnp.concatenate([zp_b[c][:, LANES:], vs[c]], axis=0),
                     jnp.concatenate([bh[c], kh[c]], axis=0)) for c in chunks]

    for c in chunks:
        rhat_sc[cur, c] = r_hat[c]
        yin_sc[cur, c] = y_in[c][:C, :] + y_in[c][C:, :]
        amat_sc[cur, c] = a_mat[c]
        dmat_sc[cur, c] = d_mat[c]
        decay_sc[cur, c] = decay_all[c * C:c * C + 8, :]


def _rwkv_scan(r, lw, k, v, kk, a, tile):
    B, T, W = r.shape
    n_pairs = W // LANES
    n_tiles = T // tile
    n_chunks = tile // CHUNK
    in_spec = pl.BlockSpec((1, tile, LANES), lambda b, h, t: (b, jnp.minimum(t, n_tiles - 1), h))
    out_spec = pl.BlockSpec((1, tile, LANES), lambda b, h, t: (b, jnp.maximum(t - 1, 0), h))
    return pl.pallas_call(
        functools.partial(_rwkv_kernel, n_chunks),
        grid=(B, n_pairs, n_tiles + 1),
        in_specs=[in_spec] * 6,
        out_specs=out_spec,
        out_shape=jax.ShapeDtypeStruct((B, T, W), F32),
        scratch_shapes=[pltpu.VMEM((LANES, LANES), F32),
                        pltpu.VMEM((2, n_chunks, 2 * CHUNK, LANES), BF16),
                        pltpu.VMEM((2, n_chunks, CHUNK, LANES), F32),
                        pltpu.VMEM((2, n_chunks, LANES, LANES), BF16),
                        pltpu.VMEM((2, n_chunks, LANES, LANES), F32),
                        pltpu.VMEM((2, n_chunks, 8, LANES), F32)],
        compiler_params=pltpu.CompilerParams(
            dimension_semantics=("arbitrary", "arbitrary", "arbitrary"),
            vmem_limit_bytes=VMEM_LIMIT),
        name="rwkv_scan",
    )(r, lw, k, v, kk, a)


SB_LOG_ZERO = -104.0
SB_ROW_PARTS = 2


def _sb_kernel(blk, q_ref, k_ref, v_ref, o_ref, acc_ref, carry_ref):
    lane = lax.broadcasted_iota(jnp.int32, (1, LANES), 1)
    in_h0 = lane < HEAD_DIM
    row = lax.broadcasted_iota(jnp.int32, (blk, blk), 0)
    col = lax.broadcasted_iota(jnp.int32, (blk, blk), 1)
    causal = col < row
    tri = jnp.where(row > col, 1.0, 0.0).astype(BF16)

    tri_sum = jnp.concatenate([tri, jnp.ones((blk, LANES), BF16)], axis=1)
    n_parts = SB_ROW_PARTS
    part = blk // n_parts
    chains = [(h, p) for h in (0, 1) for p in range(n_parts)]
    rows_of = lambda p: slice(p * part, (p + 1) * part)
    reps = blk // LANES

    def one_block(q_heads, j, masked):
        start = pl.multiple_of(j * blk, blk)
        kj = k_ref[0, pl.ds(start, blk), :]
        vj = v_ref[0, pl.ds(start, blk), :]
        z = [_dot_nt(q_heads[h][rows_of(p)], kj) for h, p in chains]
        sp = [jnp.maximum(x, 0.0) + jnp.log(1.0 + jnp.exp(-jnp.abs(x))) for x in z]
        lk = [-x for x in sp]
        if masked:
            lk = [jnp.where(causal[rows_of(p)], x, 0.0) for (h, p), x in zip(chains, lk)]
        sums = [_dot(x.astype(BF16), tri_sum) for x in lk]
        carry = [carry_ref[h, rows_of(p), :] for h, p in chains]
        att = [jnp.exp(z[c] - sp[c] + sums[c][:, :blk] + jnp.concatenate([carry[c]] * reps, axis=1))
               for c in range(len(chains))]
        if masked:
            att = [jnp.where(causal[rows_of(p)], x, 0.0) for (h, p), x in zip(chains, att)]
        new_carry = [carry[c] + sums[c][:, blk:] for c in range(len(chains))]
        for c, (h, p) in enumerate(chains):
            acc_ref[h, rows_of(p), :] += _dot(att[c].astype(BF16), vj)
            carry_ref[h, rows_of(p), :] = new_carry[c]
        top = new_carry[0]
        for x in new_carry[1:]:
            top = jnp.maximum(top, x)
        return jnp.max(top)

    def cond(state):
        j, top = state
        return jnp.logical_and(j >= 0, top > SB_LOG_ZERO)

    def query_block(qi, unused):
        q_start = pl.multiple_of(qi * blk, blk)
        q = q_ref[0, pl.ds(q_start, blk), :]
        zero = jnp.zeros_like(q)
        q_heads = (jnp.where(in_h0, q, zero), jnp.where(in_h0, zero, q))
        acc_ref[...] = jnp.zeros_like(acc_ref)
        carry_ref[...] = jnp.zeros_like(carry_ref)
        top = one_block(q_heads, qi, True)

        def body(state):
            j, _ = state
            return j - 1, one_block(q_heads, j, False)

        lax.while_loop(cond, body, (qi - 1, top))
        o_ref[0, pl.ds(q_start, blk), :] = jnp.where(in_h0, acc_ref[0], acc_ref[1])
        return unused

    lax.fori_loop(0, q_ref.shape[1] // blk, query_block, 0)


def _sb_attention(q, k, v, blk):
    B, T, W = q.shape
    n_pairs = W // LANES
    seq_spec = pl.BlockSpec((1, T, LANES), lambda b, h: (b, 0, h))
    return pl.pallas_call(
        functools.partial(_sb_kernel, blk),
        grid=(B, n_pairs),
        in_specs=[seq_spec, seq_spec, seq_spec],
        out_specs=seq_spec,
        out_shape=jax.ShapeDtypeStruct((B, T, W), F32),
        scratch_shapes=[pltpu.VMEM((2, blk, LANES), F32), pltpu.VMEM((2, blk, LANES), F32)],
        compiler_params=pltpu.CompilerParams(
            dimension_semantics=("arbitrary", "arbitrary"),
            vmem_limit_bytes=VMEM_LIMIT),
        name="sb_attention",
    )(q, k, v)


def _rms(x, g):
    return x * lax.rsqrt(jnp.mean(x * x, axis=-1, keepdims=True) + NORM_EPS) * g


def _split2(x):
    hi = x.astype(BF16)
    return hi, (x - hi.astype(F32)).astype(BF16)


def _mix_out_kernel(has_router, y_ref, bonus_ref, gate_ref, o_ref, x_ref,
                    lnx_g_ref, lnx_b_ref, sb_g_ref, w_out_ref, post_ref, pre_ref, router_ref,
                    h_o, hn_o, route_o):
    ones_bd = _head_ones(HEAD_SUM_BLOCK)
    inv_n = 1.0 / HEAD_DIM
    y = y_ref[0]
    mean = _head_sum(y, ones_bd) * inv_n
    yc = y - mean
    var = _head_sum(yc * yc, ones_bd) * inv_n
    ya = (yc * lax.rsqrt(var + LNX_EPS) * lnx_g_ref[...] + lnx_b_ref[...] + bonus_ref[0]) * gate_ref[0]
    o = o_ref[0]
    yb = o * lax.rsqrt(_head_sum(o * o, ones_bd) * inv_n + NORM_EPS) * sb_g_ref[...]
    cat = jnp.concatenate([ya, yb], axis=-1).astype(BF16)
    y2 = _dot(cat, w_out_ref[...])
    h = x_ref[0] + _rms(y2, post_ref[...])
    hn = _rms(h, pre_ref[...])
    h_o[0] = h
    if has_router:
        hn_o[...] = _slab(hn)
        hi, lo = _split2(hn)
        r_hi, r_lo = _split2(router_ref[...])
        logits = _dot(hi, r_hi) + (_dot(lo, r_hi) + _dot(hi, r_lo))
        lane = lax.broadcasted_iota(jnp.int32, logits.shape, 1)
        neg = jnp.float32(-jnp.inf)
        logits = jnp.where(lane < N_EXPERTS, logits, neg)
        m1 = jnp.max(logits, axis=-1, keepdims=True)
        i1 = jnp.min(jnp.where(logits == m1, lane, LANES), axis=-1, keepdims=True)
        rest = jnp.where(lane == i1, neg, logits)
        m2 = jnp.max(rest, axis=-1, keepdims=True)
        i2 = jnp.min(jnp.where(rest == m2, lane, LANES), axis=-1, keepdims=True)
        e2 = jnp.exp(m2 - m1)
        g1 = 1.0 / (1.0 + e2)
        route = jnp.where(lane == 0, g1, jnp.where(lane == 1, e2 * g1, 0.0))
        route = jnp.where(lane == 2, i1.astype(F32), jnp.where(lane == 3, i2.astype(F32), route))
        route_o[0] = route
    else:
        hn_o[...] = hn.astype(BF16)
        route_o[0] = jnp.zeros_like(route_o[0])


def _mix_out(y, bonus, gate, o, x, p, tile):
    B, T, D = x.shape
    has_router = 'router' in p
    row = lambda a: a.reshape(1, -1).astype(F32)
    router = _pad_cols(p['router'], LANES) if has_router else jnp.zeros((D, LANES), F32)
    full = lambda shape: pl.BlockSpec(shape, lambda b, t: (0,) * len(shape))
    tok = lambda width: pl.BlockSpec((1, tile, width), lambda b, t: (b, t, 0))
    n_t = T // tile
    if has_router:
        assert D == ROW_SLABS * LANES
        hn_spec = pl.BlockSpec((tile * ROW_SLABS, LANES), lambda b, t: (b * n_t + t, 0))
        hn_shape = jax.ShapeDtypeStruct((B * T * ROW_SLABS, LANES), F32)
    else:
        hn_spec = pl.BlockSpec((tile, D), lambda b, t: (b * n_t + t, 0))
        hn_shape = jax.ShapeDtypeStruct((B * T, D), BF16)
    return pl.pallas_call(
        functools.partial(_mix_out_kernel, has_router),
        grid=(B, n_t),
        in_specs=[tok(RWKV_W)] * 4 + [tok(D)] +
                 [full((1, RWKV_W))] * 3 + [full((D, D)), full((1, D)), full((1, D)), full((D, LANES))],
        out_specs=[tok(D), hn_spec, tok(LANES)],
        out_shape=[jax.ShapeDtypeStruct((B, T, D), F32), hn_shape,
                   jax.ShapeDtypeStruct((B, T, LANES), F32)],
        compiler_params=pltpu.CompilerParams(
            dimension_semantics=("arbitrary", "arbitrary"), vmem_limit_bytes=VMEM_LIMIT),
        name="mix_out",
    )(y, bonus, gate, o, x, row(p['lnx_g']), row(p['lnx_b']), row(p['sb_g']),
      p['w_out'].astype(BF16), row(p['mix_post']), row(p['ffn_pre']), router)


def _ffn_kernel(h_ref, hn_ref, wg_ref, wu_ref, wd_ref, post_ref, o_ref, acc_ref):
    f = pl.program_id(1)

    @pl.when(f == 0)
    def _():
        acc_ref[...] = jnp.zeros_like(acc_ref)

    hn = hn_ref[...]
    g = _dot(hn, wg_ref[...])
    u = _dot(hn, wu_ref[...])
    act = (g * jax.nn.sigmoid(g) * u).astype(BF16)
    acc_ref[...] += _dot(act, wd_ref[...])

    @pl.when(f == pl.num_programs(1) - 1)
    def _():
        o_ref[...] = h_ref[...] + _rms(acc_ref[...], post_ref[...])


def _ffn(h, hn, w_gate, w_up, w_down, post, tile_m, tile_f):
    N, D = h.shape
    F = w_gate.shape[1]
    return pl.pallas_call(
        _ffn_kernel,
        grid=(N // tile_m, F // tile_f),
        in_specs=[pl.BlockSpec((tile_m, D), lambda i, f: (i, 0)),
                  pl.BlockSpec((tile_m, D), lambda i, f: (i, 0)),
                  pl.BlockSpec((D, tile_f), lambda i, f: (0, f)),
                  pl.BlockSpec((D, tile_f), lambda i, f: (0, f)),
                  pl.BlockSpec((tile_f, D), lambda i, f: (f, 0)),
                  pl.BlockSpec((1, D), lambda i, f: (0, 0))],
        out_specs=pl.BlockSpec((tile_m, D), lambda i, f: (i, 0)),
        out_shape=jax.ShapeDtypeStruct((N, D), F32),
        scratch_shapes=[pltpu.VMEM((tile_m, D), F32)],
        compiler_params=pltpu.CompilerParams(
            dimension_semantics=("arbitrary", "arbitrary"), vmem_limit_bytes=VMEM_LIMIT),
        name="ffn",
    )(h, hn, w_gate.astype(BF16), w_up.astype(BF16), w_down.astype(BF16),
      post.reshape(1, -1).astype(F32))


GATHER_UNROLL = 8


def _scatter_rows_kernel(idx_ref, src_ref, out_ref, sem):
    n = src_ref.shape[0]

    def issue(g, carry):
        for u in range(GATHER_UNROLL):
            i = g * GATHER_UNROLL + u
            pltpu.make_async_copy(src_ref.at[i], out_ref.at[idx_ref[0, 0, i]], sem).start(
                priority=u % 2)
        return carry

    lax.fori_loop(0, n // GATHER_UNROLL, issue, 0)
    pltpu.make_async_copy(src_ref, out_ref.at[pl.ds(0, n)], sem).wait()


def _scatter_rows(src, idx, tile):
    n_src = src.shape[0] // ROW_SLABS
    n_out = idx.shape[0]
    n_tiles = n_out // tile
    src_tiles = n_src // tile
    out = pl.pallas_call(
        _scatter_rows_kernel,
        grid=(n_tiles,),
        in_specs=[pl.BlockSpec((1, 1, tile), lambda i: (i, 0, 0), memory_space=pltpu.SMEM),
                  pl.BlockSpec((tile, ROW_SLABS, LANES), lambda i: (lax.rem(i, src_tiles), 0, 0))],
        out_specs=pl.BlockSpec(memory_space=pl.ANY),
        out_shape=jax.ShapeDtypeStruct((n_out, ROW_SLABS, LANES), F32),
        scratch_shapes=[pltpu.SemaphoreType.DMA(())],
        compiler_params=pltpu.CompilerParams(
            dimension_semantics=("arbitrary",), vmem_limit_bytes=VMEM_LIMIT),
        name="scatter_rows",
    )(idx.reshape(n_tiles, 1, tile), src.reshape(n_src, ROW_SLABS, LANES))
    return out.reshape(n_out * ROW_SLABS, LANES)


def _unslab(ref, rows):
    return ref[...].reshape(rows, ROW_SLABS, LANES).reshape(rows, ROW_SLABS * LANES)


def _slab(x):
    rows = x.shape[0]
    return x.reshape(rows, ROW_SLABS, LANES).reshape(rows * ROW_SLABS, LANES)


def _moe_kernel(te_ref, nv_ref, x_ref, wg_ref, wu_ref, wd_ref, y_ref, acc_ref, xb_ref):
    i = pl.program_id(0)
    f = pl.program_id(1)
    rows = xb_ref.shape[0]
    valid = i < nv_ref[0]

    @pl.when(jnp.logical_and(valid, f == 0))
    def _():
        acc_ref[...] = jnp.zeros_like(acc_ref)
        xb_ref[...] = _unslab(x_ref, rows).astype(BF16)

    @pl.when(valid)
    def _():
        xb = xb_ref[...]
        g = _dot(xb, wg_ref[0])
        u = _dot(xb, wu_ref[0])
        act = (g * jax.nn.sigmoid(g) * u).astype(BF16)
        acc_ref[...] += _dot(act, wd_ref[0])

    @pl.when(f == pl.num_programs(1) - 1)
    def _():
        y_ref[...] = _slab(jnp.where(valid, acc_ref[...], 0.0))


def _moe_experts(x_sorted, tile_expert, n_valid, w_gate, w_up, w_down, tile_m, tile_f):
    E, D, F = w_gate.shape
    R = x_sorted.shape[0] // ROW_SLABS
    n_f = F // tile_f

    def w_col(i, f, te, nv):
        return (te[i], 0, jnp.where(i < nv[0], f, n_f - 1))

    def w_row(i, f, te, nv):
        return (te[i], jnp.where(i < nv[0], f, n_f - 1), 0)

    slab = pl.BlockSpec((tile_m * ROW_SLABS, LANES), lambda i, f, te, nv: (i, 0))
    return pl.pallas_call(
        _moe_kernel,
        grid_spec=pltpu.PrefetchScalarGridSpec(
            num_scalar_prefetch=2,
            grid=(R // tile_m, n_f),
            in_specs=[slab,
                      pl.BlockSpec((1, D, tile_f), w_col),
                      pl.BlockSpec((1, D, tile_f), w_col),
                      pl.BlockSpec((1, tile_f, D), w_row)],
            out_specs=slab,
            scratch_shapes=[pltpu.VMEM((tile_m, D), F32), pltpu.VMEM((tile_m, D), BF16)]),
        out_shape=jax.ShapeDtypeStruct((R * ROW_SLABS, LANES), F32),
        compiler_params=pltpu.CompilerParams(
            dimension_semantics=("arbitrary", "arbitrary"), vmem_limit_bytes=VMEM_LIMIT),
        name="moe_experts",
    )(tile_expert, n_valid, x_sorted, w_gate.astype(BF16), w_up.astype(BF16), w_down.astype(BF16))


def _moe_combine_kernel(idx_ref, next_idx_ref, h_ref, route_ref, post_ref, y_ref, o_ref, buf, sem):
    i = pl.program_id(0)
    tile = h_ref.shape[0]
    slot = lax.rem(i, 2)

    def start_gather(rows_ref, dst):
        def issue(g, carry):
            for u in range(GATHER_UNROLL):
                t = g * GATHER_UNROLL + u
                for choice in (0, 1):
                    pltpu.make_async_copy(y_ref.at[rows_ref[0, choice, t]],
                                          buf.at[dst, choice * tile + t],
                                          sem.at[dst]).start(priority=choice)
            return carry

        lax.fori_loop(0, tile // GATHER_UNROLL, issue, 0)

    @pl.when(i == 0)
    def _():
        start_gather(idx_ref, 0)

    @pl.when(i + 1 < pl.num_programs(0))
    def _():
        start_gather(next_idx_ref, 1 - slot)

    pltpu.make_async_copy(y_ref.at[pl.ds(0, 2 * tile)], buf.at[slot], sem.at[slot]).wait()
    y = buf[slot]
    route = route_ref[...]
    width = ROW_SLABS * LANES
    f = (route[:, 0:1] * y[:tile].reshape(tile, width)
         + route[:, 1:2] * y[tile:].reshape(tile, width))
    o_ref[...] = h_ref[...] + _rms(f, post_ref[...])


def _moe_combine(h, y_sorted, pos, route, post, tile):
    N, D = h.shape
    n_t = N // tile
    R = y_sorted.shape[0] // ROW_SLABS
    rows = pos.reshape(2, n_t, tile).transpose(1, 0, 2)
    return pl.pallas_call(
        _moe_combine_kernel,
        grid=(n_t,),
        in_specs=[pl.BlockSpec((1, 2, tile), lambda i: (i, 0, 0), memory_space=pltpu.SMEM),
                  pl.BlockSpec((1, 2, tile), lambda i: (jnp.minimum(i + 1, n_t - 1), 0, 0),
                               memory_space=pltpu.SMEM),
                  pl.BlockSpec((tile, D), lambda i: (i, 0)),
                  pl.BlockSpec((tile, LANES), lambda i: (i, 0)),
                  pl.BlockSpec((1, D), lambda i: (0, 0)),
                  pl.BlockSpec(memory_space=pl.ANY)],
        out_specs=pl.BlockSpec((tile, D), lambda i: (i, 0)),
        out_shape=jax.ShapeDtypeStruct((N, D), F32),
        scratch_shapes=[pltpu.VMEM((2, 2 * tile, ROW_SLABS, LANES), F32),
                        pltpu.SemaphoreType.DMA((2,))],
        compiler_params=pltpu.CompilerParams(
            dimension_semantics=("arbitrary",), vmem_limit_bytes=VMEM_LIMIT),
        name="moe_combine",
    )(rows, rows, h, route, post.reshape(1, -1).astype(F32),
      y_sorted.reshape(R, ROW_SLABS, LANES))


def _moe(h, hn_slabs, route, w_gate, w_up, w_down, post, tile_m, tile_f):
    N, D = h.shape
    E = w_gate.shape[0]
    n_slots = 2 * N
    R = n_slots + E * tile_m
    n_tiles = R // tile_m

    experts = jnp.concatenate([route[:, 2], route[:, 3]]).astype(jnp.int32)
    onehot = (experts[:, None] == jnp.arange(E, dtype=jnp.int32)[None, :]).astype(jnp.int32)
    ranks = jnp.cumsum(onehot, axis=0) - onehot
    counts = jnp.sum(onehot, axis=0)
    padded = ((counts + tile_m - 1) // tile_m) * tile_m
    ends = jnp.cumsum(padded)
    pos = jnp.sum(onehot * (ranks + (ends - padded)[None, :]), axis=1)
    gap_begin = jnp.concatenate([ends - padded + counts, ends[E - 1:]])
    gap_size = jnp.concatenate([padded - counts, R - ends[E - 1:]])
    gap_end_idx = jnp.cumsum(gap_size)
    j = jnp.arange(R - n_slots, dtype=jnp.int32)
    seg = jnp.sum((j[:, None] >= gap_end_idx[None, :]).astype(jnp.int32), axis=1)
    seg_hot = (seg[:, None] == jnp.arange(E + 1, dtype=jnp.int32)[None, :]).astype(jnp.int32)
    fill_rows = j + jnp.sum(seg_hot * (gap_begin - (gap_end_idx - gap_size))[None, :], axis=1)
    row_of_slot = jnp.concatenate([pos, fill_rows]).astype(jnp.int32)
    tile_start = jnp.arange(n_tiles, dtype=jnp.int32) * tile_m
    tile_expert = jnp.minimum(
        jnp.sum((ends[None, :] <= tile_start[:, None]).astype(jnp.int32), axis=1), E - 1)
    n_valid = (ends[E - 1] // tile_m).reshape(1).astype(jnp.int32)

    x_sorted = _scatter_rows(hn_slabs, row_of_slot, tile_m)
    y_sorted = _moe_experts(x_sorted, tile_expert, n_valid, w_gate, w_up, w_down, tile_m, tile_f)
    return _moe_combine(h, y_sorted, pos.astype(jnp.int32), route, post, tile_m)


MIX_TILE = 512
RWKV_TILE = 512
SB_BLOCK = 256
FFN_TILE_M = 1024
FFN_TILE_F = 896
MOE_TILE_M = 512
MOE_TILE_F = 896


def _layer(x, p, v_first):
    B, T, D = x.shape
    r, lw, k, v, kk, a, gate, bonus, q, sk, sv = _mix_in(x, p, v_first, MIX_TILE)
    y = _rwkv_scan(r, lw, k, v, kk, a, RWKV_TILE)
    o = _sb_attention(q, sk, sv, SB_BLOCK)
    h, hn, route = _mix_out(y, bonus, gate, o, x, p, MIX_TILE)
    h2 = h.reshape(B * T, D)
    if 'router' in p:
        out = _moe(h2, hn, route.reshape(B * T, LANES), p['moe_gate'], p['moe_up'], p['moe_down'],
                   p['ffn_post'], MOE_TILE_M, MOE_TILE_F)
    else:
        out = _ffn(h2, hn, p['ff_gate'], p['ff_up'], p['ff_down'], p['ffn_post'],
                   FFN_TILE_M, FFN_TILE_F)
    return out.reshape(B, T, D), (v if v_first is None else v_first)


def kernel(x, mix_pre_0, w_in_0, mu_0, decay0_0, decay_up_0, iclr0_0, iclr_up_0, gate_up_0, k_k_0, k_a_0, r_k_0, lnx_g_0, lnx_b_0, sb_g_0, w_out_0, mix_post_0, ffn_pre_0, ff_gate_0, ff_up_0, ff_down_0, ffn_post_0, mix_pre_1, w_in_1, mu_1, decay0_1, decay_up_1, iclr0_1, iclr_up_1, gate_up_1, vres0_1, vres_up_1, k_k_1, k_a_1, r_k_1, lnx_g_1, lnx_b_1, sb_g_1, w_out_1, mix_post_1, ffn_pre_1, router_1, moe_gate_1, moe_up_1, moe_down_1, ffn_post_1):
    p0 = dict(mix_pre=mix_pre_0, w_in=w_in_0, mu=mu_0, decay0=decay0_0, decay_up=decay_up_0,
              iclr0=iclr0_0, iclr_up=iclr_up_0, gate_up=gate_up_0, k_k=k_k_0, k_a=k_a_0,
              r_k=r_k_0, lnx_g=lnx_g_0, lnx_b=lnx_b_0, sb_g=sb_g_0, w_out=w_out_0,
              mix_post=mix_post_0, ffn_pre=ffn_pre_0, ff_gate=ff_gate_0, ff_up=ff_up_0,
              ff_down=ff_down_0, ffn_post=ffn_post_0)
    p1 = dict(mix_pre=mix_pre_1, w_in=w_in_1, mu=mu_1, decay0=decay0_1, decay_up=decay_up_1,
              iclr0=iclr0_1, iclr_up=iclr_up_1, gate_up=gate_up_1, vres0=vres0_1,
              vres_up=vres_up_1, k_k=k_k_1, k_a=k_a_1, r_k=r_k_1, lnx_g=lnx_g_1, lnx_b=lnx_b_1,
              sb_g=sb_g_1, w_out=w_out_1, mix_post=mix_post_1, ffn_pre=ffn_pre_1,
              router=router_1, moe_gate=moe_gate_1, moe_up=moe_up_1, moe_down=moe_down_1,
              ffn_post=ffn_post_1)
    x, v_first = _layer(x, p0, None)
    x, _ = _layer(x, p1, v_first)
    return x
```
